```python
import math
import jax, jax.numpy as jnp
from jax import lax
import numpy as np

D_MODEL = 1024
BATCH = 4
SEQ = 4096
DEPTH = 1

DN_HEADS = 4
DN_HEAD_K = 128
DN_HEAD_V = 128
DN_CONV = 4
DN_CHUNK = 64
DF_HEADS = 4
DF_HEAD = 64
DF_BLOCK = 128
ROPE_THETA = 500000.0
ROPE_DIM = DF_HEAD // 4
FFN_DIM = 2816
FFN_CONV = 3
EPS = 1e-6
POS_OFFSET_MAX = 1024

DN_QK = DN_HEADS * DN_HEAD_K
DN_V = DN_HEADS * DN_HEAD_V
DF_QK = 2 * DF_HEADS * DF_HEAD
DF_V = DF_HEADS * 2 * DF_HEAD
MIX_WIDTH = DN_V + DF_V
IN_SIZES = (DN_QK, DN_QK, DN_V, DN_V, DN_HEADS, DN_HEADS, DF_QK, DF_QK, DF_V)
IN_WIDTH = sum(IN_SIZES)
IN_SPLITS = tuple(sum(IN_SIZES[:i + 1]) for i in range(len(IN_SIZES) - 1))

kernel_name = 'hybrid_deltanet_diffattn_convglu'


def rms_norm(x, w):
    xf = x.astype(jnp.float32)
    y = xf * lax.rsqrt(jnp.mean(xf * xf, axis=-1, keepdims=True) + EPS)
    return (y * w.astype(jnp.float32)).astype(x.dtype)


def l2_norm(x):
    xf = x.astype(jnp.float32)
    return xf * lax.rsqrt(jnp.sum(xf * xf, axis=-1, keepdims=True) + EPS)


def causal_dwconv(x, w, b=None):
    K, C = w.shape
    y = lax.conv_general_dilated(
        x, w[:, None, :].astype(x.dtype), window_strides=(1,), padding=[(K - 1, 0)],
        dimension_numbers=('NWC', 'WIO', 'NWC'), feature_group_count=C)
    return y if b is None else y + b.astype(x.dtype)


def rope_cos_sin(positions):
    inv_freq = ROPE_THETA ** (-jnp.arange(0, ROPE_DIM, 2, dtype=jnp.float32) / ROPE_DIM)
    ang = positions.astype(jnp.float32)[..., None] * inv_freq
    return jnp.cos(ang), jnp.sin(ang)


def partial_rope(x, cos, sin):
    half = ROPE_DIM // 2
    xr = x[..., :ROPE_DIM].astype(jnp.float32)
    x1, x2 = xr[..., :half], xr[..., half:]
    c, s = cos[:, :, None, :], sin[:, :, None, :]
    rot = jnp.concatenate([x1 * c - x2 * s, x2 * c + x1 * s], axis=-1)
    return jnp.concatenate([rot.astype(x.dtype), x[..., ROPE_DIM:]], axis=-1)


def gated_delta_rule(q, k, v, g, beta):
    f32 = jnp.float32
    B_, S_, H, dk = q.shape
    dv = v.shape[-1]
    C = DN_CHUNK
    N = S_ // C

    def chunks(t):
        t = t.astype(f32).reshape((B_, N, C, H) + t.shape[3:])
        return jnp.moveaxis(t, 3, 1)

    q = chunks(q) * (dk ** -0.5)
    k = chunks(k)
    v = chunks(v)
    beta = chunks(beta)
    g = jnp.cumsum(chunks(g), axis=-1)
    tril = jnp.tril(jnp.ones((C, C), dtype=bool))
    strict = jnp.tril(jnp.ones((C, C), dtype=bool), -1)
    decay = jnp.exp(jnp.where(tril, g[..., :, None] - g[..., None, :], -jnp.inf))
    k_beta = k * beta[..., None]
    lower = jnp.where(strict, jnp.einsum('bhnid,bhnjd->bhnij', k_beta, k) * decay, 0.0)
    eye = jnp.eye(C, dtype=f32)
    rhs = jnp.concatenate([v * beta[..., None], k_beta * jnp.exp(g)[..., None]], axis=-1)
    sol = lax.linalg.triangular_solve(eye + lower, rhs, left_side=True, lower=True,
                                      unit_diagonal=True)
    value, k_cumdecay = sol[..., :dv], sol[..., dv:]
    intra = jnp.where(tril, jnp.einsum('bhnid,bhnjd->bhnij', q, k) * decay, 0.0)
    g_last = g[..., -1]
    q_dec = q * jnp.exp(g)[..., None]
    k_dec = k * jnp.exp(g_last[..., None] - g)[..., None]

    def step(state, inp):
        q_i, k_i, intra_i, value_i, kcd_i, gl_i = inp
        v_new = value_i - jnp.einsum('bhck,bhkv->bhcv', kcd_i, state)
        o = (jnp.einsum('bhck,bhkv->bhcv', q_i, state)
             + jnp.einsum('bhij,bhjv->bhiv', intra_i, v_new))
        state = state * jnp.exp(gl_i)[..., None, None] + jnp.einsum('bhck,bhcv->bhkv', k_i, v_new)
        return state, o

    xs = tuple(jnp.moveaxis(t, 2, 0) for t in (q_dec, k_dec, intra, value, k_cumdecay, g_last))
    _, o = lax.scan(step, jnp.zeros((B_, H, dk, dv), f32), xs)
    return jnp.transpose(o, (1, 0, 3, 2, 4)).reshape(B_, S_, H, dv)


def diff_attention(q, k, v, lam):
    f32 = jnp.float32
    B_, S_, H2, d = q.shape
    H = H2 // 2
    dvh = v.shape[-1]
    nb = S_ // DF_BLOCK
    qb = jnp.moveaxis(jnp.transpose(q, (0, 2, 1, 3)).reshape(B_, H2, nb, DF_BLOCK, d), 2, 0)
    kt = jnp.transpose(k, (0, 2, 1, 3))
    vt = jnp.transpose(v, (0, 2, 1, 3))
    key_pos = jnp.arange(S_)
    scale = d ** -0.5

    def block(args):
        q_blk, start = args
        s = jnp.einsum('bhqd,bhkd->bhqk', q_blk, kt).astype(f32) * scale
        q_pos = start + jnp.arange(DF_BLOCK)
        s = jnp.where(key_pos[None, :] <= q_pos[:, None], s, -jnp.inf)
        p = jax.nn.softmax(s, axis=-1).reshape(B_, H, 2, DF_BLOCK, S_)
        a = p[:, :, 0] - lam * p[:, :, 1]
        return jnp.einsum('bhqk,bhkv->bhqv', a.astype(vt.dtype), vt)

    o = lax.map(block, (qb, jnp.arange(nb) * DF_BLOCK))
    return jnp.transpose(o, (1, 0, 3, 2, 4)).reshape(B_, S_, H, dvh)


def setup_inputs(seed: int = 0) -> dict:
    key = jax.random.key(seed)
    ks = jax.random.split(key, 24)
    f32 = jnp.float32
    L = DEPTH

    def nrm(k, shape, scale):
        return jax.random.normal(k, shape, f32) * scale

    def gain(k, n):
        return 1.0 + 0.02 * jax.random.normal(k, (L, n), f32)

    x = jax.random.normal(ks[0], (BATCH, SEQ, D_MODEL), f32)
    offset = jax.random.randint(ks[1], (BATCH, 1), 0, POS_OFFSET_MAX, dtype=jnp.int32)
    positions = offset + jnp.arange(SEQ, dtype=jnp.int32)[None, :]
    dt = jnp.exp(jax.random.uniform(ks[2], (L, DN_HEADS), f32, math.log(1e-3), math.log(1e-1)))
    dn_dt_bias = dt + jnp.log(-jnp.expm1(-dt))
    dn_a_log = jnp.log(jax.random.uniform(ks[3], (L, DN_HEADS), f32, 1.0, 16.0))
    return {
        'x': x,
        'positions': positions,
        'norm1_w': gain(ks[4], D_MODEL),
        'w_in': nrm(ks[5], (L, D_MODEL, IN_WIDTH), D_MODEL ** -0.5),
        'dn_conv_w': nrm(ks[6], (L, DN_CONV, 2 * DN_QK + DN_V), DN_CONV ** -0.5),
        'dn_a_log': dn_a_log,
        'dn_dt_bias': dn_dt_bias,
        'dn_norm_w': gain(ks[7], DN_HEAD_V),
        'df_q_norm_w': gain(ks[8], DF_HEAD),
        'df_k_norm_w': gain(ks[9], DF_HEAD),
        'df_lambda_q1': nrm(ks[10], (L, DF_HEAD), 0.1),
        'df_lambda_k1': nrm(ks[11], (L, DF_HEAD), 0.1),
        'df_lambda_q2': nrm(ks[12], (L, DF_HEAD), 0.1),
        'df_lambda_k2': nrm(ks[13], (L, DF_HEAD), 0.1),
        'df_subln_w': gain(ks[14], 2 * DF_HEAD),
        'w_out': nrm(ks[15], (L, MIX_WIDTH, D_MODEL), MIX_WIDTH ** -0.5),
        'norm2_w': gain(ks[16], D_MODEL),
        'w_up': nrm(ks[17], (L, D_MODEL, 2 * FFN_DIM), D_MODEL ** -0.5),
        'ffn_conv_w': nrm(ks[18], (L, FFN_CONV, 2 * FFN_DIM), FFN_CONV ** -0.5),
        'ffn_conv_b': nrm(ks[19], (L, 2 * FFN_DIM), 0.02),
        'w_down': nrm(ks[20], (L, FFN_DIM, D_MODEL), FFN_DIM ** -0.5),
    }


def reference(x, positions, norm1_w, w_in, dn_conv_w, dn_a_log, dn_dt_bias, dn_norm_w,
              df_q_norm_w, df_k_norm_w, df_lambda_q1, df_lambda_k1, df_lambda_q2,
              df_lambda_k2, df_subln_w, w_out, norm2_w, w_up, ffn_conv_w, ffn_conv_b,
              w_down):
    f32 = jnp.float32
    B_, S_, _ = x.shape
    cos, sin = rope_cos_sin(positions)
    h = x
    for l in range(DEPTH):
        lam_init = 0.8 - 0.6 * math.exp(-0.3 * l)
        proj = rms_norm(h, norm1_w[l]) @ w_in[l]
        dq, dk, dv, dz, da, db, fq, fk, fv = jnp.split(proj, IN_SPLITS, axis=-1)

        qkv = jax.nn.silu(causal_dwconv(jnp.concatenate([dq, dk, dv], axis=-1), dn_conv_w[l]))
        cq, ck, cv = jnp.split(qkv, (DN_QK, 2 * DN_QK), axis=-1)
        q_a = l2_norm(cq.reshape(B_, S_, DN_HEADS, DN_HEAD_K))
        k_a = l2_norm(ck.reshape(B_, S_, DN_HEADS, DN_HEAD_K))
        v_a = cv.reshape(B_, S_, DN_HEADS, DN_HEAD_V)
        g_a = -jnp.exp(dn_a_log[l].astype(f32)) * jax.nn.softplus(
            da.astype(f32) + dn_dt_bias[l].astype(f32))
        beta_a = jax.nn.sigmoid(db.astype(f32))
        o_a = gated_delta_rule(q_a, k_a, v_a, g_a, beta_a).astype(h.dtype)
        o_a = rms_norm(o_a, dn_norm_w[l]) * jax.nn.silu(dz.reshape(B_, S_, DN_HEADS, DN_HEAD_V))

        q_b = partial_rope(rms_norm(fq.reshape(B_, S_, 2 * DF_HEADS, DF_HEAD), df_q_norm_w[l]), cos, sin)
        k_b = partial_rope(rms_norm(fk.reshape(B_, S_, 2 * DF_HEADS, DF_HEAD), df_k_norm_w[l]), cos, sin)
        v_b = fv.reshape(B_, S_, DF_HEADS, 2 * DF_HEAD)
        lam = (jnp.exp(jnp.sum(df_lambda_q1[l].astype(f32) * df_lambda_k1[l].astype(f32)))
               - jnp.exp(jnp.sum(df_lambda_q2[l].astype(f32) * df_lambda_k2[l].astype(f32)))
               + lam_init)
        o_b = diff_attention(q_b, k_b, v_b, lam)
        o_b = rms_norm(o_b, df_subln_w[l]) * (1.0 - lam_init)

        mix = jnp.concatenate([o_a.reshape(B_, S_, DN_V), o_b.reshape(B_, S_, DF_V)], axis=-1)
        h = h + mix @ w_out[l]

        u = causal_dwconv(rms_norm(h, norm2_w[l]) @ w_up[l], ffn_conv_w[l], ffn_conv_b[l])
        gate, up = jnp.split(u, 2, axis=-1)
        h = h + (jax.nn.silu(gate) * up) @ w_down[l]
    return h
```

```python
import functools
import math

import jax
import jax.numpy as jnp
from jax import lax
from jax.experimental import pallas as pl
from jax.experimental.pallas import tpu as pltpu

F32 = jnp.float32
BF16 = jnp.bfloat16

EPS = 1e-6
ROPE_THETA = 500000.0

DN_HEADS = 4
DN_HEAD = 128
DN_CONV = 4
DF_HEADS = 4
DF_HEAD = 64
ROPE_DIM = DF_HEAD // 4
ROPE_HALF = ROPE_DIM // 2
FFN_CONV = 3
HEADS_W = 512

LANES = 128
SUBLANES = 8
MXU_DIM = 256

TOK_TILE = 512
DN_TILE = 256
DN_CHUNK = 128
INV_BASE = 16
ATT_BLK = 256
FFN_BLK = 256
CONV_PAD = SUBLANES

VMEM_LIMIT_PROJ = 48 * 1024 * 1024
VMEM_LIMIT_DN = 32 * 1024 * 1024
VMEM_LIMIT_ATT = 32 * 1024 * 1024
VMEM_LIMIT_FFN = 56 * 1024 * 1024


def _dot(a, b):
    return jnp.dot(a, b, preferred_element_type=F32)


def _dot_nt(a, b):
    return lax.dot_general(a, b, (((1,), (1,)), ((), ())), preferred_element_type=F32)


def _dot_tn(a, b):
    return lax.dot_general(a, b, (((0,), (0,)), ((), ())), preferred_element_type=F32)


def _split_dot(a, b_bf16, terms):
    acc = None
    rem = a
    for _ in range(terms):
        part = rem.astype(BF16)
        d = _dot(part, b_bf16)
        acc = d if acc is None else acc + d
        rem = rem - part.astype(F32)
    return acc


def _silu(x):
    return x * jax.nn.sigmoid(x)


def _softplus(x):
    return jnp.maximum(x, 0.0) + jnp.log(1.0 + jnp.exp(-jnp.abs(x)))


def _qk_norm_rope(t, nw, cos, sin):
    n = t.shape[-1]
    t3 = t.reshape(2 * DF_HEADS, DF_HEAD, n)
    ms = jnp.mean(t3 * t3, axis=1, keepdims=True)
    t3 = t3 * lax.rsqrt(ms + EPS) * nw[None]
    x1 = t3[:, 0:ROPE_HALF, :]
    x2 = t3[:, ROPE_HALF:ROPE_DIM, :]
    c = cos[None]
    s = sin[None]
    out = jnp.concatenate([x1 * c - x2 * s, x2 * c + x1 * s, t3[:, ROPE_DIM:, :]], axis=1)
    return out.reshape(2 * DF_HEADS * DF_HEAD, n)


def _in_proj_kernel(x_ref, pos_ref, n1w_ref, wqkv_ref, wz_ref, wab_ref, wfqt_ref, wfkt_ref,
                    wfvt_ref, convw_ref, qnw_ref, knw_ref, invf_ref,
                    qa_ref, ka_ref, va_ref, z_ref, ab_ref, qt_ref, kb_ref, vt_ref,
                    ext_ref):
    tm = x_ref.shape[1]
    s_idx = pl.program_id(1)

    @pl.when(s_idx == 0)
    def _():
        ext_ref[0:CONV_PAD, :] = jnp.zeros((CONV_PAD, ext_ref.shape[1]), F32)

    @pl.when(s_idx > 0)
    def _():
        ext_ref[0:CONV_PAD, :] = ext_ref[tm:tm + CONV_PAD, :]

    x = x_ref[0]
    ms = jnp.mean(x * x, axis=-1, keepdims=True)
    xn = (x * lax.rsqrt(ms + EPS) * n1w_ref[...]).astype(BF16)

    outs = (qa_ref, ka_ref, va_ref)
    for seg in range(3):
        lo = seg * HEADS_W
        u = _dot(xn, wqkv_ref[:, lo:lo + HEADS_W])
        ext_ref[CONV_PAD:CONV_PAD + tm, lo:lo + HEADS_W] = u
        y = convw_ref[DN_CONV - 1:DN_CONV, lo:lo + HEADS_W] * u
        for j in range(DN_CONV - 1):
            off = CONV_PAD - (DN_CONV - 1) + j
            y = y + convw_ref[j:j + 1, lo:lo + HEADS_W] * ext_ref[off:off + tm, lo:lo + HEADS_W]
        y = _silu(y)
        for h in range(DN_HEADS):
            blk = y[:, h * DN_HEAD:(h + 1) * DN_HEAD]
            if seg < 2:
                ss = jnp.sum(blk * blk, axis=-1, keepdims=True)
                blk = blk * lax.rsqrt(ss + EPS)
                if seg == 0:
                    blk = blk * (DN_HEAD ** -0.5)
            outs[seg][0, :, h * DN_HEAD:(h + 1) * DN_HEAD] = blk.astype(BF16)

    z_ref[0] = _silu(_dot(xn, wz_ref[...])).astype(BF16)
    ab_ref[0] = _dot(xn, wab_ref[...])

    n_blk = tm // ATT_BLK
    pos = pos_ref[0].astype(F32)
    qt = _dot_nt(wfqt_ref[...], xn)
    kt = _dot_nt(wfkt_ref[...], xn)
    vt = _dot_nt(wfvt_ref[...], xn)
    for i in range(n_blk):
        sl = slice(i * ATT_BLK, (i + 1) * ATT_BLK)
        ang = pos[:, sl] * invf_ref[...]
        cos = jnp.cos(ang)
        sin = jnp.sin(ang)
        q_i = _qk_norm_rope(qt[:, sl], qnw_ref[...], cos, sin) * (DF_HEAD ** -0.5)
        k_i = _qk_norm_rope(kt[:, sl], knw_ref[...], cos, sin)
        qt_ref[0, i] = q_i.astype(BF16)
        kb_ref[0, sl, :] = k_i.T.astype(BF16)
        vt_ref[0, i] = vt[:, sl].astype(BF16)


def _in_proj(x, pos3, n1w, wqkv, wz, wab, wfqt, wfkt, wfvt, convw, qnw, knw, invf):
    B, S, D = x.shape
    tm = TOK_TILE
    n_blk = tm // ATT_BLK
    grid = (B, S // tm)
    full = lambda a: pl.BlockSpec(a.shape, lambda b, s: (0,) * a.ndim)
    tok = lambda w: pl.BlockSpec((1, tm, w), lambda b, s: (b, s, 0))
    tpose = pl.BlockSpec((1, n_blk, HEADS_W, ATT_BLK), lambda b, s: (b, s, 0, 0))
    slab = jax.ShapeDtypeStruct((B, S, HEADS_W), BF16)
    slab_t = jax.ShapeDtypeStruct((B, S // ATT_BLK, HEADS_W, ATT_BLK), BF16)
    return pl.pallas_call(
        _in_proj_kernel,
        grid=grid,
        in_specs=[tok(D), pl.BlockSpec((1, 1, tm), lambda b, s: (b, 0, s)), full(n1w), full(wqkv),
                  full(wz), full(wab), full(wfqt), full(wfkt), full(wfvt), full(convw), full(qnw),
                  full(knw), full(invf)],
        out_specs=[tok(HEADS_W), tok(HEADS_W), tok(HEADS_W), tok(HEADS_W), tok(LANES),
                   tpose, tok(HEADS_W), tpose],
        out_shape=[slab, slab, slab, slab, jax.ShapeDtypeStruct((B, S, LANES), F32),
                   slab_t, slab, slab_t],
        scratch_shapes=[pltpu.VMEM((tm + CONV_PAD, 3 * HEADS_W), F32)],
        compiler_params=pltpu.CompilerParams(
            dimension_semantics=("arbitrary", "arbitrary"), vmem_limit_bytes=VMEM_LIMIT_PROJ),
        name="in_proj",
    )(x, pos3, n1w, wqkv, wz, wab, wfqt, wfkt, wfvt, convw, qnw, knw, invf)


def _unit_lower_inverse(a, row, col):
    c = a.shape[0]
    eye = (row == col).astype(F32)

    def same_block(size):
        return (row // size) == (col // size)

    d = jnp.where(same_block(INV_BASE), a, 0.0)
    n = eye - d
    p = d
    width = 2
    while width < INV_BASE:
        pb = p.astype(BF16)
        p = _dot(pb, pb)
        n = n + _dot(n.astype(BF16), p.astype(BF16))
        width *= 2
    size = INV_BASE
    while size < c:
        e = jnp.where(same_block(2 * size) & jnp.logical_not(same_block(size)), a, 0.0)
        nb = n.astype(BF16)
        ne = _dot(nb, e.astype(BF16))
        n = n - _dot(ne.astype(BF16), nb)
        size *= 2
    return n, eye


def _deltanet_kernel(q_ref, k_ref, v_ref, z_ref, ab_ref, alog_ref, dtb_ref, nw_ref, o_ref, state_ref):
    tc = q_ref.shape[1]
    c = DN_CHUNK

    @pl.when(pl.program_id(1) == 0)
    def _():
        state_ref[...] = jnp.zeros(state_ref.shape, F32)

    ab = ab_ref[0]
    g = -jnp.exp(alog_ref[...]) * _softplus(ab + dtb_ref[...])
    beta = jax.nn.sigmoid(ab)
    r_t = lax.broadcasted_iota(jnp.int32, (tc, tc), 0)
    c_t = lax.broadcasted_iota(jnp.int32, (tc, tc), 1)
    tril_bd = ((r_t >= c_t) & ((r_t // c) == (c_t // c))).astype(BF16)
    gcum = _split_dot_left(tril_bd, g)
    r_e = lax.broadcasted_iota(jnp.int32, (LANES, HEADS_W), 0)
    c_e = lax.broadcasted_iota(jnp.int32, (LANES, HEADS_W), 1)
    exp_g = (r_e == c_e // DN_HEAD).astype(BF16)
    exp_b = (r_e == c_e // DN_HEAD + DN_HEADS).astype(BF16)
    gcum_b = _split_dot(gcum, exp_g, 3)
    beta_b = _split_dot(beta, exp_b, 2)
    gcum_t = gcum.T

    row = lax.broadcasted_iota(jnp.int32, (c, c), 0)
    col = lax.broadcasted_iota(jnp.int32, (c, c), 1)
    lower = row >= col
    strict = row > col

    for ci in range(tc // c):
        rs = slice(ci * c, (ci + 1) * c)
        for h in range(DN_HEADS):
            hs = slice(h * DN_HEAD, (h + 1) * DN_HEAD)
            q = q_ref[0, rs, hs]
            k = k_ref[0, rs, hs]
            v = v_ref[0, rs, hs].astype(F32)
            g_col = gcum_b[rs, hs]
            g_row = gcum_t[h:h + 1, rs]
            b_col = beta_b[rs, hs]
            decay = jnp.where(lower, jnp.exp(jnp.minimum(g_col - g_row, 0.0)), 0.0)
            kf = k.astype(F32)
            k_beta = kf * b_col
            a = jnp.where(strict, _dot_nt(k_beta.astype(BF16), k) * decay, 0.0)
            intra = _dot_nt(q, k) * decay
            t_inv, eye = _unit_lower_inverse(a, row, col)
            e_g = jnp.exp(g_col)
            rhs = jnp.concatenate([v * b_col, k_beta * e_g], axis=1)
            sol = rhs + _dot((t_inv - eye).astype(BF16), rhs.astype(BF16))
            value = sol[:, :DN_HEAD]
            k_cum = sol[:, DN_HEAD:]

            state = state_ref[h]
            state_b = state.astype(BF16)
            v_new = value - _dot(k_cum.astype(BF16), state_b)
            q_dec = (q.astype(F32) * e_g).astype(BF16)
            o = _dot(q_dec, state_b) + _dot(intra.astype(BF16), v_new.astype(BF16))
            g_last = g_col[c - 1:c, :]
            k_dec = (kf * jnp.exp(g_last - g_col)).astype(BF16)
            state_ref[h] = state * jnp.exp(g_last) + _dot_tn(k_dec, v_new.astype(BF16))

            ms = jnp.mean(o * o, axis=-1, keepdims=True)
            o = o * lax.rsqrt(ms + EPS) * nw_ref[...]
            o_ref[0, rs, hs] = (o * z_ref[0, rs, hs].astype(F32)).astype(BF16)


def _split_dot_left(m_bf16, a):
    acc = None
    rem = a
    for _ in range(3):
        part = rem.astype(BF16)
        d = _dot(m_bf16, part)
        acc = d if acc is None else acc + d
        rem = rem - part.astype(F32)
    return acc


def _deltanet(qa, ka, va, zs, ab, alog_n, dtb_n, nw):
    B, S, _ = qa.shape
    tc = DN_TILE
    tok = lambda w: pl.BlockSpec((1, tc, w), lambda b, s: (b, s, 0))
    full = lambda a: pl.BlockSpec(a.shape, lambda b, s: (0,) * a.ndim)
    return pl.pallas_call(
        _deltanet_kernel,
        grid=(B, S // tc),
        in_specs=[tok(HEADS_W), tok(HEADS_W), tok(HEADS_W), tok(HEADS_W), tok(LANES),
                  full(alog_n), full(dtb_n), full(nw)],
        out_specs=tok(HEADS_W),
        out_shape=jax.ShapeDtypeStruct((B, S, HEADS_W), BF16),
        scratch_shapes=[pltpu.VMEM((DN_HEADS, DN_HEAD, DN_HEAD), F32)],
        compiler_params=pltpu.CompilerParams(
            dimension_semantics=("arbitrary", "arbitrary"), vmem_limit_bytes=VMEM_LIMIT_DN),
        name="deltanet",
    )(qa, ka, va, zs, ab, alog_n, dtb_n, nw)


def _diff_attn_kernel(qt_ref, k_ref, vt_ref, lq1_ref, lk1_ref, lq2_ref, lk2_ref, sw_ref, o_ref,
                      *, lam_init):
    blk = ATT_BLK
    dv = 2 * DF_HEAD
    iq = pl.program_id(2)

    qt = qt_ref[0, 0]
    d_idx = lax.broadcasted_iota(jnp.int32, qt.shape, 0)
    zero = jnp.zeros_like(qt)
    q_maps = (jnp.where(d_idx < DF_HEAD, qt, zero), jnp.where(d_idx >= DF_HEAD, qt, zero))

    def step(j, carry, masked):
        k_blk = k_ref[0, pl.ds(pl.multiple_of(j * blk, blk), blk), :]
        v_blk = vt_ref[0, j]
        new = []
        for m_i in range(2):
            m_old, l_old, acc = carry[m_i]
            s = _dot(k_blk, q_maps[m_i])
            if masked:
                kpos = lax.broadcasted_iota(jnp.int32, s.shape, 0)
                qpos = lax.broadcasted_iota(jnp.int32, s.shape, 1)
                s = jnp.where(kpos <= qpos, s, -jnp.inf)
            m_new = jnp.maximum(m_old, jnp.max(s, axis=0, keepdims=True))
            alpha = jnp.exp(m_old - m_new)
            p = jnp.exp(s - m_new)
            l_new = alpha * l_old + jnp.sum(p, axis=0, keepdims=True)
            acc = alpha * acc + _dot(v_blk, p.astype(BF16))
            new.append((m_new, l_new, acc))
        return tuple(new)

    init = tuple((jnp.full((1, blk), -jnp.inf, F32), jnp.zeros((1, blk), F32),
                  jnp.zeros((dv, blk), F32)) for _ in range(2))
    carry = lax.fori_loop(0, iq, lambda j, c: step(j, c, False), init)
    (_, l1, acc1), (_, l2, acc2) = step(iq, carry, True)

    lam = (jnp.exp(jnp.sum(lq1_ref[...] * lk1_ref[...], axis=-1, keepdims=True))
           - jnp.exp(jnp.sum(lq2_ref[...] * lk2_ref[...], axis=-1, keepdims=True)) + lam_init)
    o = acc1 / l1 - lam * (acc2 / l2)
    ms = jnp.mean(o * o, axis=0, keepdims=True)
    o = o * lax.rsqrt(ms + EPS) * sw_ref[...] * (1.0 - lam_init)
    o_ref[0] = o.T.astype(BF16)


def _diff_attn(qt, kb, vt, lq1, lk1, lq2, lk2, sw, lam_init):
    B, S, _ = kb.shape
    nb = S // ATT_BLK
    dv = 2 * DF_HEAD
    full = lambda a: pl.BlockSpec(a.shape, lambda b, h, i: (0,) * a.ndim)
    return pl.pallas_call(
        functools.partial(_diff_attn_kernel, lam_init=lam_init),
        grid=(B, DF_HEADS, nb),
        in_specs=[pl.BlockSpec((1, 1, dv, ATT_BLK), lambda b, h, i: (b, i, h, 0)),
                  pl.BlockSpec((1, S, dv), lambda b, h, i: (b, 0, h)),
                  pl.BlockSpec((1, nb, dv, ATT_BLK), lambda b, h, i: (b, 0, h, 0)),
                  full(lq1), full(lk1), full(lq2), full(lk2), full(sw)],
        out_specs=pl.BlockSpec((1, ATT_BLK, dv), lambda b, h, i: (b, i, h)),
        out_shape=jax.ShapeDtypeStruct((B, S, HEADS_W), BF16),
        compiler_params=pltpu.CompilerParams(
            dimension_semantics=("arbitrary", "arbitrary", "arbitrary"),
            vmem_limit_bytes=VMEM_LIMIT_ATT),
        name="diff_attn",
    )(qt, kb, vt, lq1, lk1, lq2, lk2, sw)


def _out_ffn_kernel(x_ref, oa_ref, ob_ref, wout_ref, n2w_ref, wup_ref, cw_ref, cb_ref, wdown_ref,
                    out_ref, carry_ref, ext_ref, act_ref):
    tm = x_ref.shape[1]
    ffn = wdown_ref.shape[0]
    fb = FFN_BLK

    @pl.when(pl.program_id(1) == 0)
    def _():
        carry_ref[...] = jnp.zeros(carry_ref.shape, F32)

    mix = jnp.concatenate([oa_ref[0], ob_ref[0]], axis=-1)
    h1 = x_ref[0] + _dot(mix, wout_ref[...])
    ms = jnp.mean(h1 * h1, axis=-1, keepdims=True)
    hn = (h1 * lax.rsqrt(ms + EPS) * n2w_ref[...]).astype(BF16)

    def conv_cols(lo, slot):
        cs = slice(lo, lo + fb)
        es = slice(slot * fb, (slot + 1) * fb)
        u = _dot(hn, wup_ref[:, cs])
        ext_ref[0:CONV_PAD, es] = carry_ref[:, cs]
        ext_ref[CONV_PAD:CONV_PAD + tm, es] = u
        carry_ref[:, cs] = u[tm - CONV_PAD:tm, :]
        y = cw_ref[FFN_CONV - 1:FFN_CONV, cs] * u + cb_ref[:, cs]
        for j in range(FFN_CONV - 1):
            off = CONV_PAD - (FFN_CONV - 1) + j
            y = y + cw_ref[j:j + 1, cs] * ext_ref[off:off + tm, es]
        return y

    for ci in range(ffn // fb):
        gate = conv_cols(ci * fb, 0)
        up = conv_cols(ffn + ci * fb, 1)
        act_ref[:, ci * fb:(ci + 1) * fb] = (_silu(gate) * up).astype(BF16)

    out_ref[0] = h1 + _dot(act_ref[...], wdown_ref[...])


def _out_ffn(x, oa, ob, wout, n2w, wup, cw, cb, wdown):
    B, S, D = x.shape
    tm = TOK_TILE
    ffn = wdown.shape[0]
    tok = lambda w: pl.BlockSpec((1, tm, w), lambda b, s: (b, s, 0))
    full = lambda a: pl.BlockSpec(a.shape, lambda b, s: (0,) * a.ndim,
                                  pipeline_mode=pl.Buffered(1))
    return pl.pallas_call(
        _out_ffn_kernel,
        grid=(B, S // tm),
        in_specs=[tok(D), tok(HEADS_W), tok(HEADS_W), full(wout), full(n2w), full(wup), full(cw),
                  full(cb), full(wdown)],
        out_specs=tok(D),
        out_shape=jax.ShapeDtypeStruct((B, S, D), F32),
        scratch_shapes=[pltpu.VMEM((CONV_PAD, 2 * ffn), F32),
                        pltpu.VMEM((tm + CONV_PAD, 2 * FFN_BLK), F32),
                        pltpu.VMEM((tm, ffn), BF16)],
        compiler_params=pltpu.CompilerParams(
            dimension_semantics=("arbitrary", "arbitrary"), vmem_limit_bytes=VMEM_LIMIT_FFN),
        name="out_ffn",
    )(x, oa, ob, wout, n2w, wup, cw, cb, wdown)


def _lane_place(v, offset):
    return jnp.zeros((1, LANES), F32).at[0, offset:offset + v.shape[0]].set(v.astype(F32))


def kernel(x, positions, norm1_w, w_in, dn_conv_w, dn_a_log, dn_dt_bias, dn_norm_w, df_q_norm_w,
           df_k_norm_w, df_lambda_q1, df_lambda_k1, df_lambda_q2, df_lambda_k2, df_subln_w, w_out,
           norm2_w, w_up, ffn_conv_w, ffn_conv_b, w_down):
    B, S, D = x.shape
    depth = w_in.shape[0]
    w = HEADS_W
    h = x
    pos3 = positions.reshape(B, 1, S)
    inv_freq = ROPE_THETA ** (-jnp.arange(0, ROPE_DIM, 2, dtype=F32) / ROPE_DIM)
    invf = jnp.broadcast_to(inv_freq[:, None], (ROPE_HALF, ATT_BLK))
    for l in range(depth):
        lam_init = 0.8 - 0.6 * math.exp(-0.3 * l)
        wi = w_in[l].astype(BF16)
        wqkv = wi[:, 0:3 * w]
        wz = wi[:, 3 * w:4 * w]
        n_gate = 2 * DN_HEADS
        wab = jnp.pad(wi[:, 4 * w:4 * w + n_gate], ((0, 0), (0, LANES - n_gate)))
        base = 4 * w + n_gate
        wfqt = wi[:, base:base + w].T
        wfkt = wi[:, base + w:base + 2 * w].T
        wfvt = wi[:, base + 2 * w:base + 3 * w].T
        qnw = jnp.broadcast_to(df_q_norm_w[l].astype(F32)[:, None], (DF_HEAD, ATT_BLK))
        knw = jnp.broadcast_to(df_k_norm_w[l].astype(F32)[:, None], (DF_HEAD, ATT_BLK))

        qa, ka, va, zs, ab, qt, kb, vt = _in_proj(
            h, pos3, norm1_w[l][None, :], wqkv, wz, wab, wfqt, wfkt, wfvt, dn_conv_w[l], qnw, knw,
            invf)

        o_a = _deltanet(qa, ka, va, zs, ab, _lane_place(dn_a_log[l], 0),
                        _lane_place(dn_dt_bias[l], 0), dn_norm_w[l][None, :].astype(F32))

        sw = jnp.broadcast_to(df_subln_w[l].astype(F32)[:, None], (2 * DF_HEAD, ATT_BLK))
        o_b = _diff_attn(qt, kb, vt, df_lambda_q1[l][None, :], df_lambda_k1[l][None, :],
                         df_lambda_q2[l][None, :], df_lambda_k2[l][None, :], sw, lam_init)

        h = _out_ffn(h, o_a, o_b, w_out[l].astype(BF16), norm2_w[l][None, :],
                     w_up[l].astype(BF16), ffn_conv_w[l], ffn_conv_b[l][None, :],
                     w_down[l].astype(BF16))
    return h
```

```python
import functools
import math

import jax
import jax.numpy as jnp
from jax import lax
from jax.experimental import pallas as pl
from jax.experimental.pallas import tpu as pltpu

F32 = jnp.float32
BF16 = jnp.bfloat16

EPS = 1e-6
ROPE_THETA = 500000.0

DN_HEADS = 4
DN_HEAD = 128
DN_CONV = 4
DF_HEADS = 4
DF_HEAD = 64
ROPE_DIM = DF_HEAD // 4
ROPE_HALF = ROPE_DIM // 2
FFN_CONV = 3
HEADS_W = 512

LANES = 128
SUBLANES = 8
MXU_DIM = 256

TOK_TILE = 512
DN_TILE = 256
DN_CHUNK = 128
INV_BASE = 16
ATT_BLK = 256
FFN_BLK = 256
CONV_PAD = SUBLANES

VMEM_LIMIT_PROJ = 48 * 1024 * 1024
VMEM_LIMIT_DN = 32 * 1024 * 1024
VMEM_LIMIT_ATT = 32 * 1024 * 1024
VMEM_LIMIT_FFN = 56 * 1024 * 1024


def _dot(a, b):
    return jnp.dot(a, b, preferred_element_type=F32)


def _dot_nt(a, b):
    return lax.dot_general(a, b, (((1,), (1,)), ((), ())), preferred_element_type=F32)


def _dot_tn(a, b):
    return lax.dot_general(a, b, (((0,), (0,)), ((), ())), preferred_element_type=F32)


def _split_dot(a, b_bf16, terms):
    acc = None
    rem = a
    for _ in range(terms):
        part = rem.astype(BF16)
        d = _dot(part, b_bf16)
        acc = d if acc is None else acc + d
        rem = rem - part.astype(F32)
    return acc


def _silu(x):
    return x * jax.nn.sigmoid(x)


def _softplus(x):
    return jnp.maximum(x, 0.0) + jnp.log(1.0 + jnp.exp(-jnp.abs(x)))


def _qk_norm_rope(t, nw, cos, sin):
    n = t.shape[-1]
    t3 = t.reshape(2 * DF_HEADS, DF_HEAD, n)
    ms = jnp.mean(t3 * t3, axis=1, keepdims=True)
    t3 = t3 * lax.rsqrt(ms + EPS) * nw[None]
    x1 = t3[:, 0:ROPE_HALF, :]
    x2 = t3[:, ROPE_HALF:ROPE_DIM, :]
    c = cos[None]
    s = sin[None]
    out = jnp.concatenate([x1 * c - x2 * s, x2 * c + x1 * s, t3[:, ROPE_DIM:, :]], axis=1)
    return out.reshape(2 * DF_HEADS * DF_HEAD, n)


def _in_proj_kernel(x_ref, pos_ref, n1w_ref, wqkv_ref, wz_ref, wab_ref, wfqt_ref, wfkt_ref,
                    wfvt_ref, convw_ref, qnw_ref, knw_ref, invf_ref,
                    qa_ref, ka_ref, va_ref, z_ref, ab_ref, qt_ref, kb_ref, vt_ref,
                    ext_ref):
    tm = x_ref.shape[1]
    s_idx = pl.program_id(1)

    @pl.when(s_idx == 0)
    def _():
        ext_ref[0:CONV_PAD, :] = jnp.zeros((CONV_PAD, ext_ref.shape[1]), F32)

    @pl.when(s_idx > 0)
    def _():
        ext_ref[0:CONV_PAD, :] = ext_ref[tm:tm + CONV_PAD, :]

    x = x_ref[0]
    ms = jnp.mean(x * x, axis=-1, keepdims=True)
    xn = (x * lax.rsqrt(ms + EPS) * n1w_ref[...]).astype(BF16)

    outs = (qa_ref, ka_ref, va_ref)
    for seg in range(3):
        lo = seg * HEADS_W
        u = _dot(xn, wqkv_ref[:, lo:lo + HEADS_W])
        ext_ref[CONV_PAD:CONV_PAD + tm, lo:lo + HEADS_W] = u
        y = convw_ref[DN_CONV - 1:DN_CONV, lo:lo + HEADS_W] * u
        for j in range(DN_CONV - 1):
            off = CONV_PAD - (DN_CONV - 1) + j
            y = y + convw_ref[j:j + 1, lo:lo + HEADS_W] * ext_ref[off:off + tm, lo:lo + HEADS_W]
        y = _silu(y)
        for h in range(DN_HEADS):
            blk = y[:, h * DN_HEAD:(h + 1) * DN_HEAD]
            if seg < 2:
                ss = jnp.sum(blk * blk, axis=-1, keepdims=True)
                blk = blk * lax.rsqrt(ss + EPS)
                if seg == 0:
                    blk = blk * (DN_HEAD ** -0.5)
            outs[seg][0, :, h * DN_HEAD:(h + 1) * DN_HEAD] = blk.astype(BF16)

    z_ref[0] = _silu(_dot(xn, wz_ref[...])).astype(BF16)
    ab_ref[0] = _dot(xn, wab_ref[...])

    n_blk = tm // ATT_BLK
    pos = pos_ref[0].astype(F32)
    qt = _dot_nt(wfqt_ref[...], xn)
    kt = _dot_nt(wfkt_ref[...], xn)
    vt = _dot_nt(wfvt_ref[...], xn)
    for i in range(n_blk):
        sl = slice(i * ATT_BLK, (i + 1) * ATT_BLK)
        ang = pos[:, sl] * invf_ref[...]
        cos = jnp.cos(ang)
        sin = jnp.sin(ang)
        q_i = _qk_norm_rope(qt[:, sl], qnw_ref[...], cos, sin) * (DF_HEAD ** -0.5)
        k_i = _qk_norm_rope(kt[:, sl], knw_ref[...], cos, sin)
        qt_ref[0, i] = q_i.astype(BF16)
        kb_ref[0, sl, :] = k_i.T.astype(BF16)
        vt_ref[0, i] = vt[:, sl].astype(BF16)


def _in_proj(x, pos3, n1w, wqkv, wz, wab, wfqt, wfkt, wfvt, convw, qnw, knw, invf):
    B, S, D = x.shape
    tm = TOK_TILE
    n_blk = tm // ATT_BLK
    grid = (B, S // tm)
    full = lambda a: pl.BlockSpec(a.shape, lambda b, s: (0,) * a.ndim)
    tok = lambda w: pl.BlockSpec((1, tm, w), lambda b, s: (b, s, 0))
    tpose = pl.BlockSpec((1, n_blk, HEADS_W, ATT_BLK), lambda b, s: (b, s, 0, 0))
    slab = jax.ShapeDtypeStruct((B, S, HEADS_W), BF16)
    slab_t = jax.ShapeDtypeStruct((B, S // ATT_BLK, HEADS_W, ATT_BLK), BF16)
    return pl.pallas_call(
        _in_proj_kernel,
        grid=grid,
        in_specs=[tok(D), pl.BlockSpec((1, 1, tm), lambda b, s: (b, 0, s)), full(n1w), full(wqkv),
                  full(wz), full(wab), full(wfqt), full(wfkt), full(wfvt), full(convw), full(qnw),
                  full(knw), full(invf)],
        out_specs=[tok(HEADS_W), tok(HEADS_W), tok(HEADS_W), tok(HEADS_W), tok(LANES),
                   tpose, tok(HEADS_W), tpose],
        out_shape=[slab, slab, slab, slab, jax.ShapeDtypeStruct((B, S, LANES), F32),
                   slab_t, slab, slab_t],
        scratch_shapes=[pltpu.VMEM((tm + CONV_PAD, 3 * HEADS_W), F32)],
        compiler_params=pltpu.CompilerParams(
            dimension_semantics=("arbitrary", "arbitrary"), vmem_limit_bytes=VMEM_LIMIT_PROJ),
        name="in_proj",
    )(x, pos3, n1w, wqkv, wz, wab, wfqt, wfkt, wfvt, convw, qnw, knw, invf)


def _unit_lower_inverse(mats, row, col):
    c = mats[0].shape[0]
    eye = (row == col).astype(F32)

    def same_block(size):
        return (row // size) == (col // size)

    base = same_block(INV_BASE)
    ps = [jnp.where(base, a, 0.0) for a in mats]
    ns = [eye - d for d in ps]
    width = 2
    while width < INV_BASE:
        pbs = [p.astype(BF16) for p in ps]
        ps = [_dot(pb, pb) for pb in pbs]
        ns = [n + _dot(n.astype(BF16), p.astype(BF16)) for n, p in zip(ns, ps)]
        width *= 2
    size = INV_BASE
    while size < c:
        join = same_block(2 * size) & jnp.logical_not(same_block(size))
        nbs = [n.astype(BF16) for n in ns]
        nes = [_dot(nb, jnp.where(join, a, 0.0).astype(BF16)) for nb, a in zip(nbs, mats)]
        ns = [n - _dot(ne.astype(BF16), nb) for n, ne, nb in zip(ns, nes, nbs)]
        size *= 2
    return [n - eye for n in ns]


def _deltanet_kernel(q_ref, k_ref, v_ref, z_ref, ab_ref, alog_ref, dtb_ref, nw_ref, o_ref, state_ref):
    tc = q_ref.shape[1]
    c = DN_CHUNK

    @pl.when(pl.program_id(1) == 0)
    def _():
        state_ref[...] = jnp.zeros(state_ref.shape, F32)

    ab = ab_ref[0]
    g = -jnp.exp(alog_ref[...]) * _softplus(ab + dtb_ref[...])
    beta = jax.nn.sigmoid(ab)
    r_t = lax.broadcasted_iota(jnp.int32, (tc, tc), 0)
    c_t = lax.broadcasted_iota(jnp.int32, (tc, tc), 1)
    tril_bd = ((r_t >= c_t) & ((r_t // c) == (c_t // c))).astype(BF16)
    gcum = _split_dot_left(tril_bd, g)
    r_e = lax.broadcasted_iota(jnp.int32, (LANES, HEADS_W), 0)
    c_e = lax.broadcasted_iota(jnp.int32, (LANES, HEADS_W), 1)
    exp_g = (r_e == c_e // DN_HEAD).astype(BF16)
    exp_b = (r_e == c_e // DN_HEAD + DN_HEADS).astype(BF16)
    gcum_b = _split_dot(gcum, exp_g, 3)
    beta_b = _split_dot(beta, exp_b, 2)
    gcum_t = gcum.T

    row = lax.broadcasted_iota(jnp.int32, (c, c), 0)
    col = lax.broadcasted_iota(jnp.int32, (c, c), 1)
    lower = row >= col
    strict = row > col

    tiles = [(slice(ci * c, (ci + 1) * c), h, slice(h * DN_HEAD, (h + 1) * DN_HEAD))
             for ci in range(tc // c) for h in range(DN_HEADS)]
    qs = [q_ref[0, rs, hs] for rs, _, hs in tiles]
    ks = [k_ref[0, rs, hs] for rs, _, hs in tiles]
    g_cols = [gcum_b[rs, hs] for rs, _, hs in tiles]
    b_cols = [beta_b[rs, hs] for rs, _, hs in tiles]
    decays = [jnp.where(lower, jnp.exp(jnp.minimum(g_col - gcum_t[h:h + 1, rs], 0.0)), 0.0)
              for g_col, (rs, h, _) in zip(g_cols, tiles)]
    k_betas = [k.astype(F32) * b_col for k, b_col in zip(ks, b_cols)]
    a_mats = [jnp.where(strict, _dot_nt(kb.astype(BF16), k) * dec, 0.0)
              for kb, k, dec in zip(k_betas, ks, decays)]
    intras = [(_dot_nt(q, k) * dec).astype(BF16) for q, k, dec in zip(qs, ks, decays)]
    t_offs = _unit_lower_inverse(a_mats, row, col)
    e_gs = [jnp.exp(g_col) for g_col in g_cols]
    rhss = [jnp.concatenate([v_ref[0, rs, hs].astype(F32) * b_col, kb * e_g], axis=1)
            for (rs, _, hs), b_col, kb, e_g in zip(tiles, b_cols, k_betas, e_gs)]
    sols = [rhs + _dot(t.astype(BF16), rhs.astype(BF16)) for t, rhs in zip(t_offs, rhss)]
    q_decs = [(q.astype(F32) * e_g).astype(BF16) for q, e_g in zip(qs, e_gs)]
    g_lasts = [g_col[c - 1:c, :] for g_col in g_cols]
    k_decs = [(k.astype(F32) * jnp.exp(g_last - g_col)).astype(BF16)
              for k, g_last, g_col in zip(ks, g_lasts, g_cols)]

    for ci in range(tc // c):
        idx = [ci * DN_HEADS + h for h in range(DN_HEADS)]
        states = [state_ref[h] for h in range(DN_HEADS)]
        state_bs = [st.astype(BF16) for st in states]
        v_news = [sols[i][:, :DN_HEAD] - _dot(sols[i][:, DN_HEAD:].astype(BF16), sb)
                  for i, sb in zip(idx, state_bs)]
        inter = [_dot(q_decs[i], sb) for i, sb in zip(idx, state_bs)]
        v_new_bs = [vn.astype(BF16) for vn in v_news]
        outs = [o + _dot(intras[i], vb) for i, o, vb in zip(idx, inter, v_new_bs)]
        for h, i in enumerate(idx):
            state_ref[h] = states[h] * jnp.exp(g_lasts[i]) + _dot_tn(k_decs[i], v_new_bs[h])
        for h, i in enumerate(idx):
            rs, _, hs = tiles[i]
            o = outs[h]
            ms = jnp.mean(o * o, axis=-1, keepdims=True)
            o = o * lax.rsqrt(ms + EPS) * nw_ref[...]
            o_ref[0, rs, hs] = (o * z_ref[0, rs, hs].astype(F32)).astype(BF16)


def _split_dot_left(m_bf16, a):
    acc = None
    rem = a
    for _ in range(3):
        part = rem.astype(BF16)
        d = _dot(m_bf16, part)
        acc = d if acc is None else acc + d
        rem = rem - part.astype(F32)
    return acc


def _deltanet(qa, ka, va, zs, ab, alog_n, dtb_n, nw):
    B, S, _ = qa.shape
    tc = DN_TILE
    tok = lambda w: pl.BlockSpec((1, tc, w), lambda b, s: (b, s, 0))
    full = lambda a: pl.BlockSpec(a.shape, lambda b, s: (0,) * a.ndim)
    return pl.pallas_call(
        _deltanet_kernel,
        grid=(B, S // tc),
        in_specs=[tok(HEADS_W), tok(HEADS_W), tok(HEADS_W), tok(HEADS_W), tok(LANES),
                  full(alog_n), full(dtb_n), full(nw)],
        out_specs=tok(HEADS_W),
        out_shape=jax.ShapeDtypeStruct((B, S, HEADS_W), BF16),
        scratch_shapes=[pltpu.VMEM((DN_HEADS, DN_HEAD, DN_HEAD), F32)],
        compiler_params=pltpu.CompilerParams(
            dimension_semantics=("arbitrary", "arbitrary"), vmem_limit_bytes=VMEM_LIMIT_DN),
        name="deltanet",
    )(qa, ka, va, zs, ab, alog_n, dtb_n, nw)


def _diff_attn_kernel(qt_ref, k_ref, vt_ref, lq1_ref, lk1_ref, lq2_ref, lk2_ref, sw_ref, o_ref,
                      sa_ref, sb_ref, m_ref, l_ref, acc_ref, *, lam_init):
    blk = ATT_BLK
    dv = 2 * DF_HEAD
    iq = pl.program_id(2)

    qt = qt_ref[0, 0]
    d_idx = lax.broadcasted_iota(jnp.int32, qt.shape, 0)
    zero = jnp.zeros_like(qt)
    q_maps = (jnp.where(d_idx < DF_HEAD, qt, zero), jnp.where(d_idx >= DF_HEAD, qt, zero))

    def scores(j, s_ref):
        k_blk = k_ref[0, pl.ds(pl.multiple_of(j * blk, blk), blk), :]
        for m_i in range(2):
            s_ref[m_i] = _dot(k_blk, q_maps[m_i])

    def consume(j, s_ref, masked):
        v_blk = vt_ref[0, j]
        for m_i in range(2):
            s = s_ref[m_i]
            if masked:
                kpos = lax.broadcasted_iota(jnp.int32, s.shape, 0)
                qpos = lax.broadcasted_iota(jnp.int32, s.shape, 1)
                s = jnp.where(kpos <= qpos, s, -jnp.inf)
            m_old = m_ref[m_i]
            m_new = jnp.maximum(m_old, jnp.max(s, axis=0, keepdims=True))
            alpha = jnp.exp(m_old - m_new)
            p = jnp.exp(s - m_new)
            l_ref[m_i] = alpha * l_ref[m_i] + jnp.sum(p, axis=0, keepdims=True)
            m_ref[m_i] = m_new
            acc_ref[m_i] = alpha * acc_ref[m_i] + _dot(v_blk, p.astype(BF16))

    m_ref[...] = jnp.full(m_ref.shape, -jnp.inf, F32)
    l_ref[...] = jnp.zeros(l_ref.shape, F32)
    acc_ref[...] = jnp.zeros(acc_ref.shape, F32)
    scores(0, sa_ref)

    def pair(t, _):
        a = 2 * t
        scores(a + 1, sb_ref)
        consume(a, sa_ref, False)
        scores(a + 2, sa_ref)
        consume(a + 1, sb_ref, False)
        return 0

    lax.fori_loop(0, iq // 2, pair, 0)

    @pl.when(iq % 2 == 0)
    def _():
        consume(iq, sa_ref, True)

    @pl.when(iq % 2 == 1)
    def _():
        scores(iq, sb_ref)
        consume(iq - 1, sa_ref, False)
        consume(iq, sb_ref, True)

    acc1, acc2 = acc_ref[0], acc_ref[1]
    l1, l2 = l_ref[0], l_ref[1]
    lam = (jnp.exp(jnp.sum(lq1_ref[...] * lk1_ref[...], axis=-1, keepdims=True))
           - jnp.exp(jnp.sum(lq2_ref[...] * lk2_ref[...], axis=-1, keepdims=True)) + lam_init)
    o = acc1 / l1 - lam * (acc2 / l2)
    ms = jnp.mean(o * o, axis=0, keepdims=True)
    o = o * lax.rsqrt(ms + EPS) * sw_ref[...] * (1.0 - lam_init)
    o_ref[0] = o.T.astype(BF16)


def _diff_attn(qt, kb, vt, lq1, lk1, lq2, lk2, sw, lam_init):
    B, S, _ = kb.shape
    nb = S // ATT_BLK
    dv = 2 * DF_HEAD
    full = lambda a: pl.BlockSpec(a.shape, lambda b, h, i: (0,) * a.ndim)
    return pl.pallas_call(
        functools.partial(_diff_attn_kernel, lam_init=lam_init),
        grid=(B, DF_HEADS, nb),
        in_specs=[pl.BlockSpec((1, 1, dv, ATT_BLK), lambda b, h, i: (b, i, h, 0)),
                  pl.BlockSpec((1, S, dv), lambda b, h, i: (b, 0, h)),
                  pl.BlockSpec((1, nb, dv, ATT_BLK), lambda b, h, i: (b, 0, h, 0)),
                  full(lq1), full(lk1), full(lq2), full(lk2), full(sw)],
        out_specs=pl.BlockSpec((1, ATT_BLK, dv), lambda b, h, i: (b, i, h)),
        out_shape=jax.ShapeDtypeStruct((B, S, HEADS_W), BF16),
        scratch_shapes=[pltpu.VMEM((2, ATT_BLK, ATT_BLK), F32), pltpu.VMEM((2, ATT_BLK, ATT_BLK), F32),
                        pltpu.VMEM((2, 1, ATT_BLK), F32), pltpu.VMEM((2, 1, ATT_BLK), F32),
                        pltpu.VMEM((2, dv, ATT_BLK), F32)],
        compiler_params=pltpu.CompilerParams(
            dimension_semantics=("arbitrary", "arbitrary", "arbitrary"),
            vmem_limit_bytes=VMEM_LIMIT_ATT),
        name="diff_attn",
    )(qt, kb, vt, lq1, lk1, lq2, lk2, sw)


def _out_ffn_kernel(x_ref, oa_ref, ob_ref, wout_ref, n2w_ref, wup_ref, cw_ref, cb_ref, wdown_ref,
                    out_ref, carry_ref, ext_ref, act_ref):
    tm = x_ref.shape[1]
    ffn = wdown_ref.shape[0]
    fb = FFN_BLK

    @pl.when(pl.program_id(1) == 0)
    def _():
        carry_ref[...] = jnp.zeros(carry_ref.shape, F32)

    mix = jnp.concatenate([oa_ref[0], ob_ref[0]], axis=-1)
    h1 = x_ref[0] + _dot(mix, wout_ref[...])
    ms = jnp.mean(h1 * h1, axis=-1, keepdims=True)
    hn = (h1 * lax.rsqrt(ms + EPS) * n2w_ref[...]).astype(BF16)

    def conv_cols(lo, slot):
        cs = slice(lo, lo + fb)
        es = slice(slot * fb, (slot + 1) * fb)
        u = _dot(hn, wup_ref[:, cs])
        ext_ref[0:CONV_PAD, es] = carry_ref[:, cs]
        ext_ref[CONV_PAD:CONV_PAD + tm, es] = u
        carry_ref[:, cs] = u[tm - CONV_PAD:tm, :]
        y = cw_ref[FFN_CONV - 1:FFN_CONV, cs] * u + cb_ref[:, cs]
        for j in range(FFN_CONV - 1):
            off = CONV_PAD - (FFN_CONV - 1) + j
            y = y + cw_ref[j:j + 1, cs] * ext_ref[off:off + tm, es]
        return y

    for ci in range(ffn // fb):
        gate = conv_cols(ci * fb, 0)
        up = conv_cols(ffn + ci * fb, 1)
        act_ref[:, ci * fb:(ci + 1) * fb] = (_silu(gate) * up).astype(BF16)

    out_ref[0] = h1 + _dot(act_ref[...], wdown_ref[...])


def _out_ffn(x, oa, ob, wout, n2w, wup, cw, cb, wdown):
    B, S, D = x.shape
    tm = TOK_TILE
    ffn = wdown.shape[0]
    tok = lambda w: pl.BlockSpec((1, tm, w), lambda b, s: (b, s, 0))
    full = lambda a: pl.BlockSpec(a.shape, lambda b, s: (0,) * a.ndim,
                                  pipeline_mode=pl.Buffered(1))
    return pl.pallas_call(
        _out_ffn_kernel,
        grid=(B, S // tm),
        in_specs=[tok(D), tok(HEADS_W), tok(HEADS_W), full(wout), full(n2w), full(wup), full(cw),
                  full(cb), full(wdown)],
        out_specs=tok(D),
        out_shape=jax.ShapeDtypeStruct((B, S, D), F32),
        scratch_shapes=[pltpu.VMEM((CONV_PAD, 2 * ffn), F32),
                        pltpu.VMEM((tm + CONV_PAD, 2 * FFN_BLK), F32),
                        pltpu.VMEM((tm, ffn), BF16)],
        compiler_params=pltpu.CompilerParams(
            dimension_semantics=("arbitrary", "arbitrary"), vmem_limit_bytes=VMEM_LIMIT_FFN),
        name="out_ffn",
    )(x, oa, ob, wout, n2w, wup, cw, cb, wdown)


def _lane_place(v, offset):
    return jnp.zeros((1, LANES), F32).at[0, offset:offset + v.shape[0]].set(v.astype(F32))


def kernel(x, positions, norm1_w, w_in, dn_conv_w, dn_a_log, dn_dt_bias, dn_norm_w, df_q_norm_w,
           df_k_norm_w, df_lambda_q1, df_lambda_k1, df_lambda_q2, df_lambda_k2, df_subln_w, w_out,
           norm2_w, w_up, ffn_conv_w, ffn_conv_b, w_down):
    B, S, D = x.shape
    depth = w_in.shape[0]
    w = HEADS_W
    h = x
    pos3 = positions.reshape(B, 1, S)
    inv_freq = ROPE_THETA ** (-jnp.arange(0, ROPE_DIM, 2, dtype=F32) / ROPE_DIM)
    invf = jnp.broadcast_to(inv_freq[:, None], (ROPE_HALF, ATT_BLK))
    for l in range(depth):
        lam_init = 0.8 - 0.6 * math.exp(-0.3 * l)
        wi = w_in[l].astype(BF16)
        wqkv = wi[:, 0:3 * w]
        wz = wi[:, 3 * w:4 * w]
        n_gate = 2 * DN_HEADS
        wab = jnp.pad(wi[:, 4 * w:4 * w + n_gate], ((0, 0), (0, LANES - n_gate)))
        base = 4 * w + n_gate
        wfqt = wi[:, base:base + w].T
        wfkt = wi[:, base + w:base + 2 * w].T
        wfvt = wi[:, base + 2 * w:base + 3 * w].T
        qnw = jnp.broadcast_to(df_q_norm_w[l].astype(F32)[:, None], (DF_HEAD, ATT_BLK))
        knw = jnp.broadcast_to(df_k_norm_w[l].astype(F32)[:, None], (DF_HEAD, ATT_BLK))

        qa, ka, va, zs, ab, qt, kb, vt = _in_proj(
            h, pos3, norm1_w[l][None, :], wqkv, wz, wab, wfqt, wfkt, wfvt, dn_conv_w[l], qnw, knw,
            invf)

        o_a = _deltanet(qa, ka, va, zs, ab, _lane_place(dn_a_log[l], 0),
                        _lane_place(dn_dt_bias[l], 0), dn_norm_w[l][None, :].astype(F32))

        sw = jnp.broadcast_to(df_subln_w[l].astype(F32)[:, None], (2 * DF_HEAD, ATT_BLK))
        o_b = _diff_attn(qt, kb, vt, df_lambda_q1[l][None, :], df_lambda_k1[l][None, :],
                         df_lambda_q2[l][None, :], df_lambda_k2[l][None, :], sw, lam_init)

        h = _out_ffn(h, o_a, o_b, w_out[l].astype(BF16), norm2_w[l][None, :],
                     w_up[l].astype(BF16), ffn_conv_w[l], ffn_conv_b[l][None, :],
                     w_down[l].astype(BF16))
    return h
```

```python
import functools
import math

import jax
import jax.numpy as jnp
from jax import lax
from jax.experimental import pallas as pl
from jax.experimental.pallas import tpu as pltpu

F32 = jnp.float32
BF16 = jnp.bfloat16

EPS = 1e-6
ROPE_THETA = 500000.0

DN_HEADS = 4
DN_HEAD = 128
DN_CONV = 4
DF_HEADS = 4
DF_HEAD = 64
ROPE_DIM = DF_HEAD // 4
ROPE_HALF = ROPE_DIM // 2
FFN_CONV = 3
HEADS_W = 512

LANES = 128
SUBLANES = 8
MXU_DIM = 256

TOK_TILE = 512
DN_TILE = 256
DN_CHUNK = 128
DN_SEQS = 2
INV_BASE = 16
ATT_BLK = 256
FFN_BLK = 256
CONV_PAD = SUBLANES
ATT_HEADS_PER_STEP = 4
DV = 2 * DF_HEAD
BF16_SUBLANES = 2 * SUBLANES
DV_AUG = DV + BF16_SUBLANES
LOG2E = math.log2(math.e)

VMEM_LIMIT_PROJ = 48 * 1024 * 1024
VMEM_LIMIT_DN = 32 * 1024 * 1024
VMEM_LIMIT_ATT = 40 * 1024 * 1024
VMEM_LIMIT_FFN = 56 * 1024 * 1024


def _dot(a, b):
    return jnp.dot(a, b, preferred_element_type=F32)


def _dot_nt(a, b):
    return lax.dot_general(a, b, (((1,), (1,)), ((), ())), preferred_element_type=F32)


def _dot_tn(a, b):
    return lax.dot_general(a, b, (((0,), (0,)), ((), ())), preferred_element_type=F32)


def _split_dot(a, b_bf16, terms):
    acc = None
    rem = a
    for _ in range(terms):
        part = rem.astype(BF16)
        d = _dot(part, b_bf16)
        acc = d if acc is None else acc + d
        rem = rem - part.astype(F32)
    return acc


def _silu(x):
    return x * jax.nn.sigmoid(x)


def _softplus(x):
    return jnp.maximum(x, 0.0) + jnp.log(1.0 + jnp.exp(-jnp.abs(x)))


def _qk_norm_rope(t, nw, cos, sin):
    n = t.shape[-1]
    t3 = t.reshape(2 * DF_HEADS, DF_HEAD, n)
    ms = jnp.mean(t3 * t3, axis=1, keepdims=True)
    t3 = t3 * lax.rsqrt(ms + EPS) * nw[None]
    x1 = t3[:, 0:ROPE_HALF, :]
    x2 = t3[:, ROPE_HALF:ROPE_DIM, :]
    c = cos[None]
    s = sin[None]
    out = jnp.concatenate([x1 * c - x2 * s, x2 * c + x1 * s, t3[:, ROPE_DIM:, :]], axis=1)
    return out.reshape(2 * DF_HEADS * DF_HEAD, n)


def _in_proj_kernel(x_ref, pos_ref, n1w_ref, wqkv_ref, wz_ref, wab_ref, wfqt_ref, wfkt_ref,
                    wfvt_ref, convw_ref, qnw_ref, knw_ref, invf_ref,
                    qa_ref, ka_ref, va_ref, z_ref, ab_ref, qt_ref, kb_ref, vt_ref,
                    ext_ref):
    tm = x_ref.shape[1]
    s_idx = pl.program_id(1)

    @pl.when(s_idx == 0)
    def _():
        ext_ref[0:CONV_PAD, :] = jnp.zeros((CONV_PAD, ext_ref.shape[1]), F32)

    @pl.when(s_idx > 0)
    def _():
        ext_ref[0:CONV_PAD, :] = ext_ref[tm:tm + CONV_PAD, :]

    x = x_ref[0]
    ms = jnp.mean(x * x, axis=-1, keepdims=True)
    xn = (x * lax.rsqrt(ms + EPS) * n1w_ref[...]).astype(BF16)

    outs = (qa_ref, ka_ref, va_ref)
    t_weights = (wfqt_ref, wfkt_ref, wfvt_ref)
    t_proj = []
    for seg in range(3):
        lo = seg * HEADS_W
        u = _dot(xn, wqkv_ref[:, lo:lo + HEADS_W])
        ext_ref[CONV_PAD:CONV_PAD + tm, lo:lo + HEADS_W] = u
        t_proj.append(_dot_nt(t_weights[seg][...], xn))
        y = convw_ref[DN_CONV - 1:DN_CONV, lo:lo + HEADS_W] * u
        for j in range(DN_CONV - 1):
            off = CONV_PAD - (DN_CONV - 1) + j
            y = y + convw_ref[j:j + 1, lo:lo + HEADS_W] * ext_ref[off:off + tm, lo:lo + HEADS_W]
        y = _silu(y)
        for h in range(DN_HEADS):
            blk = y[:, h * DN_HEAD:(h + 1) * DN_HEAD]
            if seg < 2:
                ss = jnp.sum(blk * blk, axis=-1, keepdims=True)
                blk = blk * lax.rsqrt(ss + EPS)
                if seg == 0:
                    blk = blk * (DN_HEAD ** -0.5)
            outs[seg][0, :, h * DN_HEAD:(h + 1) * DN_HEAD] = blk.astype(BF16)

    z_ref[0] = _silu(_dot(xn, wz_ref[...])).astype(BF16)
    ab_ref[0] = _dot(xn, wab_ref[...])

    n_blk = tm // ATT_BLK
    pos = pos_ref[0].astype(F32)
    qt, kt, vt = t_proj
    for i in range(n_blk):
        sl = slice(i * ATT_BLK, (i + 1) * ATT_BLK)
        ang = pos[:, sl] * invf_ref[...]
        cos = jnp.cos(ang)
        sin = jnp.sin(ang)
        q_i = _qk_norm_rope(qt[:, sl], qnw_ref[...], cos, sin) * (DF_HEAD ** -0.5 * LOG2E)
        k_i = _qk_norm_rope(kt[:, sl], knw_ref[...], cos, sin)
        qt_ref[0, i] = q_i.astype(BF16)
        kb_ref[0, sl, :] = k_i.T.astype(BF16)
        ones_rows = (lax.broadcasted_iota(jnp.int32, (BF16_SUBLANES, ATT_BLK), 0) == 0).astype(BF16)
        for h in range(DF_HEADS):
            vt_ref[0, i, h * DV_AUG:h * DV_AUG + DV, :] = vt[h * DV:(h + 1) * DV, sl].astype(BF16)
            vt_ref[0, i, h * DV_AUG + DV:(h + 1) * DV_AUG, :] = ones_rows


def _in_proj(x, pos3, n1w, wqkv, wz, wab, wfqt, wfkt, wfvt, convw, qnw, knw, invf):
    B, S, D = x.shape
    tm = TOK_TILE
    n_blk = tm // ATT_BLK
    grid = (B, S // tm)
    full = lambda a: pl.BlockSpec(a.shape, lambda b, s: (0,) * a.ndim)
    tok = lambda w: pl.BlockSpec((1, tm, w), lambda b, s: (b, s, 0))
    tpose = pl.BlockSpec((1, n_blk, HEADS_W, ATT_BLK), lambda b, s: (b, s, 0, 0))
    slab = jax.ShapeDtypeStruct((B, S, HEADS_W), BF16)
    slab_t = jax.ShapeDtypeStruct((B, S // ATT_BLK, HEADS_W, ATT_BLK), BF16)
    v_rows = DF_HEADS * DV_AUG
    tpose_v = pl.BlockSpec((1, n_blk, v_rows, ATT_BLK), lambda b, s: (b, s, 0, 0))
    slab_tv = jax.ShapeDtypeStruct((B, S // ATT_BLK, v_rows, ATT_BLK), BF16)
    return pl.pallas_call(
        _in_proj_kernel,
        grid=grid,
        in_specs=[tok(D), pl.BlockSpec((1, 1, tm), lambda b, s: (b, 0, s)), full(n1w), full(wqkv),
                  full(wz), full(wab), full(wfqt), full(wfkt), full(wfvt), full(convw), full(qnw),
                  full(knw), full(invf)],
        out_specs=[tok(HEADS_W), tok(HEADS_W), tok(HEADS_W), tok(HEADS_W), tok(LANES),
                   tpose, tok(HEADS_W), tpose_v],
        out_shape=[slab, slab, slab, slab, jax.ShapeDtypeStruct((B, S, LANES), F32),
                   slab_t, slab, slab_tv],
        scratch_shapes=[pltpu.VMEM((tm + CONV_PAD, 3 * HEADS_W), F32)],
        compiler_params=pltpu.CompilerParams(
            dimension_semantics=("arbitrary", "arbitrary"), vmem_limit_bytes=VMEM_LIMIT_PROJ),
        name="in_proj",
    )(x, pos3, n1w, wqkv, wz, wab, wfqt, wfkt, wfvt, convw, qnw, knw, invf)


def _unit_lower_inverse(mats, row, col):
    c = mats[0].shape[0]
    eye = (row == col).astype(F32)

    def same_block(size):
        return (row // size) == (col // size)

    base = same_block(INV_BASE)
    ps = [jnp.where(base, a, 0.0) for a in mats]
    ns = [eye - d for d in ps]
    width = 2
    while width < INV_BASE:
        pbs = [p.astype(BF16) for p in ps]
        ps = [_dot(pb, pb) for pb in pbs]
        ns = [n + _dot(n.astype(BF16), p.astype(BF16)) for n, p in zip(ns, ps)]
        width *= 2
    size = INV_BASE
    while size < c:
        join = same_block(2 * size) & jnp.logical_not(same_block(size))
        nbs = [n.astype(BF16) for n in ns]
        nes = [_dot(nb, jnp.where(join, a, 0.0).astype(BF16)) for nb, a in zip(nbs, mats)]
        ns = [n - _dot(ne.astype(BF16), nb) for n, ne, nb in zip(ns, nes, nbs)]
        size *= 2
    return [n - eye for n in ns]


def _deltanet_kernel(q_ref, k_ref, v_ref, z_ref, ab_ref, alog_ref, dtb_ref, nw_ref, o_ref, state_ref):
    tc = q_ref.shape[1]
    c = DN_CHUNK

    @pl.when(pl.program_id(1) == 0)
    def _():
        state_ref[...] = jnp.zeros(state_ref.shape, F32)

    n_seq = q_ref.shape[0]
    r_t = lax.broadcasted_iota(jnp.int32, (tc, tc), 0)
    c_t = lax.broadcasted_iota(jnp.int32, (tc, tc), 1)
    tril_bd = ((r_t >= c_t) & ((r_t // c) == (c_t // c))).astype(BF16)
    r_e = lax.broadcasted_iota(jnp.int32, (LANES, HEADS_W), 0)
    c_e = lax.broadcasted_iota(jnp.int32, (LANES, HEADS_W), 1)
    exp_g = (r_e == c_e // DN_HEAD).astype(BF16)
    exp_b = (r_e == c_e // DN_HEAD + DN_HEADS).astype(BF16)
    gcum_b, beta_b, gcum_t = [], [], []
    for bi in range(n_seq):
        ab = ab_ref[bi]
        g = -jnp.exp(alog_ref[...]) * _softplus(ab + dtb_ref[...])
        beta = jax.nn.sigmoid(ab)
        gcum = _split_dot_left(tril_bd, g)
        gcum_b.append(_split_dot(gcum, exp_g, 3))
        beta_b.append(_split_dot(beta, exp_b, 2))
        gcum_t.append(gcum.T)

    row = lax.broadcasted_iota(jnp.int32, (c, c), 0)
    col = lax.broadcasted_iota(jnp.int32, (c, c), 1)
    lower = row >= col
    strict = row > col

    chains = [(bi, h) for bi in range(n_seq) for h in range(DN_HEADS)]
    tiles = [(bi, slice(ci * c, (ci + 1) * c), h, slice(h * DN_HEAD, (h + 1) * DN_HEAD))
             for ci in range(tc // c) for bi, h in chains]
    qs = [q_ref[bi, rs, hs] for bi, rs, _, hs in tiles]
    ks = [k_ref[bi, rs, hs] for bi, rs, _, hs in tiles]
    g_cols = [gcum_b[bi][rs, hs] for bi, rs, _, hs in tiles]
    b_cols = [beta_b[bi][rs, hs] for bi, rs, _, hs in tiles]
    decays = [jnp.where(lower, jnp.exp(jnp.minimum(g_col - gcum_t[bi][h:h + 1, rs], 0.0)), 0.0)
              for g_col, (bi, rs, h, _) in zip(g_cols, tiles)]
    k_betas = [k.astype(F32) * b_col for k, b_col in zip(ks, b_cols)]
    a_mats = [jnp.where(strict, _dot_nt(kb.astype(BF16), k) * dec, 0.0)
              for kb, k, dec in zip(k_betas, ks, decays)]
    intras = [(_dot_nt(q, k) * dec).astype(BF16) for q, k, dec in zip(qs, ks, decays)]
    t_offs = _unit_lower_inverse(a_mats, row, col)
    e_gs = [jnp.exp(g_col) for g_col in g_cols]
    rhss = [jnp.concatenate([v_ref[bi, rs, hs].astype(F32) * b_col, kb * e_g], axis=1)
            for (bi, rs, _, hs), b_col, kb, e_g in zip(tiles, b_cols, k_betas, e_gs)]
    sols = [rhs + _dot(t.astype(BF16), rhs.astype(BF16)) for t, rhs in zip(t_offs, rhss)]
    q_decs = [(q.astype(F32) * e_g).astype(BF16) for q, e_g in zip(qs, e_gs)]
    g_lasts = [g_col[c - 1:c, :] for g_col in g_cols]
    k_decs = [(k.astype(F32) * jnp.exp(g_last - g_col)).astype(BF16)
              for k, g_last, g_col in zip(ks, g_lasts, g_cols)]

    n_ch = len(chains)
    for ci in range(tc // c):
        idx = [ci * n_ch + j for j in range(n_ch)]
        states = [state_ref[bi, h] for bi, h in chains]
        state_bs = [st.astype(BF16) for st in states]
        v_news = [sols[i][:, :DN_HEAD] - _dot(sols[i][:, DN_HEAD:].astype(BF16), sb)
                  for i, sb in zip(idx, state_bs)]
        inter = [_dot(q_decs[i], sb) for i, sb in zip(idx, state_bs)]
        v_new_bs = [vn.astype(BF16) for vn in v_news]
        outs = [o + _dot(intras[i], vb) for i, o, vb in zip(idx, inter, v_new_bs)]
        for j, i in enumerate(idx):
            bi, h = chains[j]
            state_ref[bi, h] = states[j] * jnp.exp(g_lasts[i]) + _dot_tn(k_decs[i], v_new_bs[j])
        for j, i in enumerate(idx):
            bi, rs, _, hs = tiles[i]
            o = outs[j]
            ms = jnp.mean(o * o, axis=-1, keepdims=True)
            o = o * lax.rsqrt(ms + EPS) * nw_ref[...]
            o_ref[bi, rs, hs] = (o * z_ref[bi, rs, hs].astype(F32)).astype(BF16)


def _split_dot_left(m_bf16, a):
    acc = None
    rem = a
    for _ in range(3):
        part = rem.astype(BF16)
        d = _dot(m_bf16, part)
        acc = d if acc is None else acc + d
        rem = rem - part.astype(F32)
    return acc


def _deltanet(qa, ka, va, zs, ab, alog_n, dtb_n, nw):
    B, S, _ = qa.shape
    tc = DN_TILE
    n_seq = DN_SEQS if B % DN_SEQS == 0 else 1
    tok = lambda w: pl.BlockSpec((n_seq, tc, w), lambda b, s: (b, s, 0))
    full = lambda a: pl.BlockSpec(a.shape, lambda b, s: (0,) * a.ndim)
    return pl.pallas_call(
        _deltanet_kernel,
        grid=(B // n_seq, S // tc),
        in_specs=[tok(HEADS_W), tok(HEADS_W), tok(HEADS_W), tok(HEADS_W), tok(LANES),
                  full(alog_n), full(dtb_n), full(nw)],
        out_specs=tok(HEADS_W),
        out_shape=jax.ShapeDtypeStruct((B, S, HEADS_W), BF16),
        scratch_shapes=[pltpu.VMEM((n_seq, DN_HEADS, DN_HEAD, DN_HEAD), F32)],
        compiler_params=pltpu.CompilerParams(
            dimension_semantics=("arbitrary", "arbitrary"), vmem_limit_bytes=VMEM_LIMIT_DN),
        name="deltanet",
    )(qa, ka, va, zs, ab, alog_n, dtb_n, nw)


def _diff_attn_kernel(qt_ref, k_ref, vt_ref, lq1_ref, lk1_ref, lq2_ref, lk2_ref, sw_ref, o_ref,
                      sa_ref, sb_ref, m_ref, acc_ref, *, lam_init):
    blk = ATT_BLK
    iq = pl.program_id(2)
    hd = 2 * DF_HEAD

    q_maps = []
    for hh in range(ATT_HEADS_PER_STEP):
        qt = qt_ref[0, 0, hh * hd:(hh + 1) * hd, :]
        d_idx = lax.broadcasted_iota(jnp.int32, qt.shape, 0)
        zero = jnp.zeros_like(qt)
        q_maps.append(jnp.where(d_idx < DF_HEAD, qt, zero))
        q_maps.append(jnp.where(d_idx >= DF_HEAD, qt, zero))

    def scores(j, s_ref):
        rows = pl.ds(pl.multiple_of(j * blk, blk), blk)
        for hh in range(ATT_HEADS_PER_STEP):
            k_blk = k_ref[0, rows, hh * hd:(hh + 1) * hd]
            for m_i in range(2):
                c = 2 * hh + m_i
                s_ref[c] = _dot(k_blk, q_maps[c])

    def consume(j, s_ref, masked):
        for hh in range(ATT_HEADS_PER_STEP):
            v_blk = vt_ref[0, j, hh * DV_AUG:(hh + 1) * DV_AUG, :]
            for m_i in range(2):
                c = 2 * hh + m_i
                s = s_ref[c]
                if masked:
                    kpos = lax.broadcasted_iota(jnp.int32, s.shape, 0)
                    qpos = lax.broadcasted_iota(jnp.int32, s.shape, 1)
                    s = jnp.where(kpos <= qpos, s, -jnp.inf)
                m_old = m_ref[c]
                m_new = jnp.maximum(m_old, jnp.max(s, axis=0, keepdims=True))
                alpha = jnp.exp2(m_old - m_new)
                p = jnp.exp2(s - m_new).astype(BF16)
                m_ref[c] = m_new
                acc_ref[c] = alpha * acc_ref[c] + _dot(v_blk, p)

    m_ref[...] = jnp.full(m_ref.shape, -jnp.inf, F32)
    acc_ref[...] = jnp.zeros(acc_ref.shape, F32)
    scores(0, sa_ref)

    def pair(t, _):
        a = 2 * t
        scores(a + 1, sb_ref)
        consume(a, sa_ref, False)
        scores(a + 2, sa_ref)
        consume(a + 1, sb_ref, False)
        return 0

    lax.fori_loop(0, iq // 2, pair, 0)

    @pl.when(iq % 2 == 0)
    def _():
        consume(iq, sa_ref, True)

    @pl.when(iq % 2 == 1)
    def _():
        scores(iq, sb_ref)
        consume(iq - 1, sa_ref, False)
        consume(iq, sb_ref, True)

    lam = (jnp.exp(jnp.sum(lq1_ref[...] * lk1_ref[...], axis=-1, keepdims=True))
           - jnp.exp(jnp.sum(lq2_ref[...] * lk2_ref[...], axis=-1, keepdims=True)) + lam_init)
    for hh in range(ATT_HEADS_PER_STEP):
        a1 = acc_ref[2 * hh]
        a2 = acc_ref[2 * hh + 1]
        inv1 = 1.0 / a1[DV:DV + 1]
        inv2 = lam / a2[DV:DV + 1]
        o = a1[:DV] * inv1 - a2[:DV] * inv2
        ms = jnp.mean(o * o, axis=0, keepdims=True)
        o = o * lax.rsqrt(ms + EPS) * sw_ref[...] * (1.0 - lam_init)
        o_ref[0, :, hh * DV:(hh + 1) * DV] = o.T.astype(BF16)


def _diff_attn(qt, kb, vt, lq1, lk1, lq2, lk2, sw, lam_init):
    B, S, _ = kb.shape
    nb = S // ATT_BLK
    hps = ATT_HEADS_PER_STEP
    chains = 2 * hps
    full = lambda a: pl.BlockSpec(a.shape, lambda b, g, i: (0,) * a.ndim)
    return pl.pallas_call(
        functools.partial(_diff_attn_kernel, lam_init=lam_init),
        grid=(B, DF_HEADS // hps, nb),
        in_specs=[pl.BlockSpec((1, 1, hps * DV, ATT_BLK), lambda b, g, i: (b, i, g, 0)),
                  pl.BlockSpec((1, S, hps * DV), lambda b, g, i: (b, 0, g)),
                  pl.BlockSpec((1, nb, hps * DV_AUG, ATT_BLK), lambda b, g, i: (b, 0, g, 0)),
                  full(lq1), full(lk1), full(lq2), full(lk2), full(sw)],
        out_specs=pl.BlockSpec((1, ATT_BLK, hps * DV), lambda b, g, i: (b, i, g)),
        out_shape=jax.ShapeDtypeStruct((B, S, HEADS_W), BF16),
        scratch_shapes=[pltpu.VMEM((chains, ATT_BLK, ATT_BLK), F32),
                        pltpu.VMEM((chains, ATT_BLK, ATT_BLK), F32),
                        pltpu.VMEM((chains, 1, ATT_BLK), F32),
                        pltpu.VMEM((chains, DV_AUG, ATT_BLK), F32)],
        compiler_params=pltpu.CompilerParams(
            dimension_semantics=("arbitrary", "arbitrary", "arbitrary"),
            vmem_limit_bytes=VMEM_LIMIT_ATT),
        name="diff_attn",
    )(qt, kb, vt, lq1, lk1, lq2, lk2, sw)


def _out_ffn_kernel(x_ref, oa_ref, ob_ref, wout_ref, n2w_ref, wup_ref, cw_ref, cb_ref, wdown_ref,
                    out_ref, carry_ref, ext_ref, act_ref):
    tm = x_ref.shape[1]
    ffn = wdown_ref.shape[0]
    fb = FFN_BLK

    @pl.when(pl.program_id(1) == 0)
    def _():
        carry_ref[...] = jnp.zeros(carry_ref.shape, F32)

    mix = jnp.concatenate([oa_ref[0], ob_ref[0]], axis=-1)
    h1 = x_ref[0] + _dot(mix, wout_ref[...])
    ms = jnp.mean(h1 * h1, axis=-1, keepdims=True)
    hn = (h1 * lax.rsqrt(ms + EPS) * n2w_ref[...]).astype(BF16)

    def conv_cols(lo, slot):
        cs = slice(lo, lo + fb)
        es = slice(slot * fb, (slot + 1) * fb)
        u = _dot(hn, wup_ref[:, cs])
        ext_ref[0:CONV_PAD, es] = carry_ref[:, cs]
        ext_ref[CONV_PAD:CONV_PAD + tm, es] = u
        carry_ref[:, cs] = u[tm - CONV_PAD:tm, :]
        y = cw_ref[FFN_CONV - 1:FFN_CONV, cs] * u + cb_ref[:, cs]
        for j in range(FFN_CONV - 1):
            off = CONV_PAD - (FFN_CONV - 1) + j
            y = y + cw_ref[j:j + 1, cs] * ext_ref[off:off + tm, es]
        return y

    for ci in range(ffn // fb):
        gate = conv_cols(ci * fb, 0)
        up = conv_cols(ffn + ci * fb, 1)
        act_ref[:, ci * fb:(ci + 1) * fb] = (_silu(gate) * up).astype(BF16)

    out_ref[0] = h1 + _dot(act_ref[...], wdown_ref[...])


def _out_ffn(x, oa, ob, wout, n2w, wup, cw, cb, wdown):
    B, S, D = x.shape
    tm = TOK_TILE
    ffn = wdown.shape[0]
    tok = lambda w: pl.BlockSpec((1, tm, w), lambda b, s: (b, s, 0))
    full = lambda a: pl.BlockSpec(a.shape, lambda b, s: (0,) * a.ndim,
                                  pipeline_mode=pl.Buffered(1))
    return pl.pallas_call(
        _out_ffn_kernel,
        grid=(B, S // tm),
        in_specs=[tok(D), tok(HEADS_W), tok(HEADS_W), full(wout), full(n2w), full(wup), full(cw),
                  full(cb), full(wdown)],
        out_specs=tok(D),
        out_shape=jax.ShapeDtypeStruct((B, S, D), F32),
        scratch_shapes=[pltpu.VMEM((CONV_PAD, 2 * ffn), F32),
                        pltpu.VMEM((tm + CONV_PAD, 2 * FFN_BLK), F32),
                        pltpu.VMEM((tm, ffn), BF16)],
        compiler_params=pltpu.CompilerParams(
            dimension_semantics=("arbitrary", "arbitrary"), vmem_limit_bytes=VMEM_LIMIT_FFN),
        name="out_ffn",
    )(x, oa, ob, wout, n2w, wup, cw, cb, wdown)


def _lane_place(v, offset):
    return jnp.zeros((1, LANES), F32).at[0, offset:offset + v.shape[0]].set(v.astype(F32))


def kernel(x, positions, norm1_w, w_in, dn_conv_w, dn_a_log, dn_dt_bias, dn_norm_w, df_q_norm_w,
           df_k_norm_w, df_lambda_q1, df_lambda_k1, df_lambda_q2, df_lambda_k2, df_subln_w, w_out,
           norm2_w, w_up, ffn_conv_w, ffn_conv_b, w_down):
    B, S, D = x.shape
    depth = w_in.shape[0]
    w = HEADS_W
    h = x
    pos3 = positions.reshape(B, 1, S)
    inv_freq = ROPE_THETA ** (-jnp.arange(0, ROPE_DIM, 2, dtype=F32) / ROPE_DIM)
    invf = jnp.broadcast_to(inv_freq[:, None], (ROPE_HALF, ATT_BLK))
    for l in range(depth):
        lam_init = 0.8 - 0.6 * math.exp(-0.3 * l)
        wi = w_in[l].astype(BF16)
        wqkv = wi[:, 0:3 * w]
        wz = wi[:, 3 * w:4 * w]
        n_gate = 2 * DN_HEADS
        wab = jnp.pad(wi[:, 4 * w:4 * w + n_gate], ((0, 0), (0, LANES - n_gate)))
        base = 4 * w + n_gate
        wfqt = wi[:, base:base + w].T
        wfkt = wi[:, base + w:base + 2 * w].T
        wfvt = wi[:, base + 2 * w:base + 3 * w].T
        qnw = jnp.broadcast_to(df_q_norm_w[l].astype(F32)[:, None], (DF_HEAD, ATT_BLK))
        knw = jnp.broadcast_to(df_k_norm_w[l].astype(F32)[:, None], (DF_HEAD, ATT_BLK))

        qa, ka, va, zs, ab, qt, kb, vt = _in_proj(
            h, pos3, norm1_w[l][None, :], wqkv, wz, wab, wfqt, wfkt, wfvt, dn_conv_w[l], qnw, knw,
            invf)

        o_a = _deltanet(qa, ka, va, zs, ab, _lane_place(dn_a_log[l], 0),
                        _lane_place(dn_dt_bias[l], 0), dn_norm_w[l][None, :].astype(F32))

        sw = jnp.broadcast_to(df_subln_w[l].astype(F32)[:, None], (2 * DF_HEAD, ATT_BLK))
        o_b = _diff_attn(qt, kb, vt, df_lambda_q1[l][None, :], df_lambda_k1[l][None, :],
                         df_lambda_q2[l][None, :], df_lambda_k2[l][None, :], sw, lam_init)

        h = _out_ffn(h, o_a, o_b, w_out[l].astype(BF16), norm2_w[l][None, :],
                     w_up[l].astype(BF16), ffn_conv_w[l], ffn_conv_b[l][None, :],
                     w_down[l].astype(BF16))
    return h
```

```python
import functools
import math

import jax
import jax.numpy as jnp
from jax import lax
from jax.experimental import pallas as pl
from jax.experimental.pallas import tpu as pltpu

F32 = jnp.float32
BF16 = jnp.bfloat16

EPS = 1e-6
ROPE_THETA = 500000.0

DN_HEADS = 4
DN_HEAD = 128
DN_CONV = 4
DF_HEADS = 4
DF_HEAD = 64
ROPE_DIM = DF_HEAD // 4
ROPE_HALF = ROPE_DIM // 2
FFN_CONV = 3
HEADS_W = 512

LANES = 128
SUBLANES = 8
MXU_DIM = 256

TOK_TILE = 512
FFN_TOK_TILE = 512
DN_TILE = 256
DN_CHUNK = 128
DN_SEQS = 2
INV_BASE = 16
ATT_BLK = 256
ATT_QBLK = 2 * ATT_BLK
FFN_BLK = 256
CONV_PAD = SUBLANES
ATT_HEADS_PER_STEP = 4
DV = 2 * DF_HEAD
BF16_SUBLANES = 2 * SUBLANES
DV_AUG = DV + BF16_SUBLANES
LOG2E = math.log2(math.e)

VMEM_LIMIT_PROJ = 48 * 1024 * 1024
VMEM_LIMIT_DN = 32 * 1024 * 1024
VMEM_LIMIT_ATT = 40 * 1024 * 1024
VMEM_LIMIT_FFN = 56 * 1024 * 1024


def _dot(a, b):
    return jnp.dot(a, b, preferred_element_type=F32)


def _dot_nt(a, b):
    return lax.dot_general(a, b, (((1,), (1,)), ((), ())), preferred_element_type=F32)


def _dot_tn(a, b):
    return lax.dot_general(a, b, (((0,), (0,)), ((), ())), preferred_element_type=F32)


def _split_dot(a, b_bf16, terms):
    acc = None
    rem = a
    for _ in range(terms):
        part = rem.astype(BF16)
        d = _dot(part, b_bf16)
        acc = d if acc is None else acc + d
        rem = rem - part.astype(F32)
    return acc


def _silu(x):
    return x * jax.nn.sigmoid(x)


def _softplus(x):
    return jnp.maximum(x, 0.0) + jnp.log(1.0 + jnp.exp(-jnp.abs(x)))


def _qk_norm_rope(t, nw, cos, sin):
    n = t.shape[-1]
    t3 = t.reshape(2 * DF_HEADS, DF_HEAD, n)
    ms = jnp.mean(t3 * t3, axis=1, keepdims=True)
    t3 = t3 * lax.rsqrt(ms + EPS) * nw[None]
    x1 = t3[:, 0:ROPE_HALF, :]
    x2 = t3[:, ROPE_HALF:ROPE_DIM, :]
    c = cos[None]
    s = sin[None]
    out = jnp.concatenate([x1 * c - x2 * s, x2 * c + x1 * s, t3[:, ROPE_DIM:, :]], axis=1)
    return out.reshape(2 * DF_HEADS * DF_HEAD, n)


def _in_proj_kernel(x_ref, pos_ref, n1w_ref, wqkv_ref, wz_ref, wab_ref, wfqt_ref, wfkt_ref,
                    wfvt_ref, convw_ref, qnw_ref, knw_ref, invf_ref,
                    qa_ref, ka_ref, va_ref, z_ref, ab_ref, qt_ref, kb_ref, vt_ref,
                    ext_ref):
    tm = x_ref.shape[1]
    s_idx = pl.program_id(1)

    @pl.when(s_idx == 0)
    def _():
        ext_ref[0:CONV_PAD, :] = jnp.zeros((CONV_PAD, ext_ref.shape[1]), F32)

    @pl.when(s_idx > 0)
    def _():
        ext_ref[0:CONV_PAD, :] = ext_ref[tm:tm + CONV_PAD, :]

    x = x_ref[0]
    ms = jnp.mean(x * x, axis=-1, keepdims=True)
    xn = (x * lax.rsqrt(ms + EPS) * n1w_ref[...]).astype(BF16)

    outs = (qa_ref, ka_ref, va_ref)
    t_weights = (wfqt_ref, wfkt_ref, wfvt_ref)
    t_proj = []
    for seg in range(3):
        lo = seg * HEADS_W
        u = _dot(xn, wqkv_ref[:, lo:lo + HEADS_W])
        ext_ref[CONV_PAD:CONV_PAD + tm, lo:lo + HEADS_W] = u
        t_proj.append(_dot_nt(t_weights[seg][...], xn))
        y = convw_ref[DN_CONV - 1:DN_CONV, lo:lo + HEADS_W] * u
        for j in range(DN_CONV - 1):
            off = CONV_PAD - (DN_CONV - 1) + j
            y = y + convw_ref[j:j + 1, lo:lo + HEADS_W] * ext_ref[off:off + tm, lo:lo + HEADS_W]
        y = _silu(y)
        for h in range(DN_HEADS):
            blk = y[:, h * DN_HEAD:(h + 1) * DN_HEAD]
            if seg < 2:
                ss = jnp.sum(blk * blk, axis=-1, keepdims=True)
                blk = blk * lax.rsqrt(ss + EPS)
                if seg == 0:
                    blk = blk * (DN_HEAD ** -0.5)
            outs[seg][0, :, h * DN_HEAD:(h + 1) * DN_HEAD] = blk.astype(BF16)

    z_ref[0] = _silu(_dot(xn, wz_ref[...])).astype(BF16)
    ab_ref[0] = _dot(xn, wab_ref[...])

    n_blk = tm // ATT_BLK
    pos = pos_ref[0].astype(F32)
    qt, kt, vt = t_proj
    for i in range(n_blk):
        sl = slice(i * ATT_BLK, (i + 1) * ATT_BLK)
        ang = pos[:, sl] * invf_ref[...]
        cos = jnp.cos(ang)
        sin = jnp.sin(ang)
        q_i = _qk_norm_rope(qt[:, sl], qnw_ref[...], cos, sin) * (DF_HEAD ** -0.5 * LOG2E)
        k_i = _qk_norm_rope(kt[:, sl], knw_ref[...], cos, sin)
        qt_ref[0, i] = q_i.astype(BF16)
        kb_ref[0, sl, :] = k_i.T.astype(BF16)
        ones_rows = (lax.broadcasted_iota(jnp.int32, (BF16_SUBLANES, ATT_BLK), 0) == 0).astype(BF16)
        for h in range(DF_HEADS):
            vt_ref[0, i, h * DV_AUG:h * DV_AUG + DV, :] = vt[h * DV:(h + 1) * DV, sl].astype(BF16)
            vt_ref[0, i, h * DV_AUG + DV:(h + 1) * DV_AUG, :] = ones_rows


def _in_proj(x, pos3, n1w, wqkv, wz, wab, wfqt, wfkt, wfvt, convw, qnw, knw, invf):
    B, S, D = x.shape
    tm = TOK_TILE
    n_blk = tm // ATT_BLK
    grid = (B, S // tm)
    full = lambda a: pl.BlockSpec(a.shape, lambda b, s: (0,) * a.ndim)
    tok = lambda w: pl.BlockSpec((1, tm, w), lambda b, s: (b, s, 0))
    tpose = pl.BlockSpec((1, n_blk, HEADS_W, ATT_BLK), lambda b, s: (b, s, 0, 0))
    slab = jax.ShapeDtypeStruct((B, S, HEADS_W), BF16)
    slab_t = jax.ShapeDtypeStruct((B, S // ATT_BLK, HEADS_W, ATT_BLK), BF16)
    v_rows = DF_HEADS * DV_AUG
    tpose_v = pl.BlockSpec((1, n_blk, v_rows, ATT_BLK), lambda b, s: (b, s, 0, 0))
    slab_tv = jax.ShapeDtypeStruct((B, S // ATT_BLK, v_rows, ATT_BLK), BF16)
    return pl.pallas_call(
        _in_proj_kernel,
        grid=grid,
        in_specs=[tok(D), pl.BlockSpec((1, 1, tm), lambda b, s: (b, 0, s)), full(n1w), full(wqkv),
                  full(wz), full(wab), full(wfqt), full(wfkt), full(wfvt), full(convw), full(qnw),
                  full(knw), full(invf)],
        out_specs=[tok(HEADS_W), tok(HEADS_W), tok(HEADS_W), tok(HEADS_W), tok(LANES),
                   tpose, tok(HEADS_W), tpose_v],
        out_shape=[slab, slab, slab, slab, jax.ShapeDtypeStruct((B, S, LANES), F32),
                   slab_t, slab, slab_tv],
        scratch_shapes=[pltpu.VMEM((tm + CONV_PAD, 3 * HEADS_W), F32)],
        compiler_params=pltpu.CompilerParams(
            dimension_semantics=("arbitrary", "arbitrary"), vmem_limit_bytes=VMEM_LIMIT_PROJ),
        name="in_proj",
    )(x, pos3, n1w, wqkv, wz, wab, wfqt, wfkt, wfvt, convw, qnw, knw, invf)


def _unit_lower_inverse(mats, row, col):
    c = mats[0].shape[0]
    eye = (row == col).astype(F32)

    def same_block(size):
        return (row // size) == (col // size)

    base = same_block(INV_BASE)
    ps = [jnp.where(base, a, 0.0) for a in mats]
    ns = [eye - d for d in ps]
    width = 2
    while width < INV_BASE:
        pbs = [p.astype(BF16) for p in ps]
        ps = [_dot(pb, pb) for pb in pbs]
        ns = [n + _dot(n.astype(BF16), p.astype(BF16)) for n, p in zip(ns, ps)]
        width *= 2
    size = INV_BASE
    while size < c:
        join = same_block(2 * size) & jnp.logical_not(same_block(size))
        nbs = [n.astype(BF16) for n in ns]
        nes = [_dot(nb, jnp.where(join, a, 0.0).astype(BF16)) for nb, a in zip(nbs, mats)]
        ns = [n - _dot(ne.astype(BF16), nb) for n, ne, nb in zip(ns, nes, nbs)]
        size *= 2
    return [n - eye for n in ns]


def _deltanet_kernel(q_ref, k_ref, v_ref, z_ref, ab_ref, alog_ref, dtb_ref, nw_ref, o_ref, state_ref):
    tc = q_ref.shape[1]
    c = DN_CHUNK

    @pl.when(pl.program_id(1) == 0)
    def _():
        state_ref[...] = jnp.zeros(state_ref.shape, F32)

    n_seq = q_ref.shape[0]
    r_t = lax.broadcasted_iota(jnp.int32, (tc, tc), 0)
    c_t = lax.broadcasted_iota(jnp.int32, (tc, tc), 1)
    tril_bd = ((r_t >= c_t) & ((r_t // c) == (c_t // c))).astype(BF16)
    r_e = lax.broadcasted_iota(jnp.int32, (LANES, HEADS_W), 0)
    c_e = lax.broadcasted_iota(jnp.int32, (LANES, HEADS_W), 1)
    exp_g = (r_e == c_e // DN_HEAD).astype(BF16)
    exp_b = (r_e == c_e // DN_HEAD + DN_HEADS).astype(BF16)
    gcum_b, beta_b, gcum_t = [], [], []
    for bi in range(n_seq):
        ab = ab_ref[bi]
        g = -jnp.exp(alog_ref[...]) * _softplus(ab + dtb_ref[...])
        beta = jax.nn.sigmoid(ab)
        gcum = _split_dot_left(tril_bd, g)
        gcum_b.append(_split_dot(gcum, exp_g, 2))
        beta_b.append(_split_dot(beta, exp_b, 1))
        gcum_t.append(gcum.T)

    row = lax.broadcasted_iota(jnp.int32, (c, c), 0)
    col = lax.broadcasted_iota(jnp.int32, (c, c), 1)
    lower = row >= col
    strict = row > col

    chains = [(bi, h) for bi in range(n_seq) for h in range(DN_HEADS)]
    tiles = [(bi, slice(ci * c, (ci + 1) * c), h, slice(h * DN_HEAD, (h + 1) * DN_HEAD))
             for ci in range(tc // c) for bi, h in chains]
    qs = [q_ref[bi, rs, hs] for bi, rs, _, hs in tiles]
    ks = [k_ref[bi, rs, hs] for bi, rs, _, hs in tiles]
    g_cols = [gcum_b[bi][rs, hs] for bi, rs, _, hs in tiles]
    b_cols = [beta_b[bi][rs, hs] for bi, rs, _, hs in tiles]
    decays = [jnp.where(lower, jnp.exp(jnp.minimum(g_col - gcum_t[bi][h:h + 1, rs], 0.0)), 0.0)
              for g_col, (bi, rs, h, _) in zip(g_cols, tiles)]
    k_betas = [k.astype(F32) * b_col for k, b_col in zip(ks, b_cols)]
    a_mats = [jnp.where(strict, _dot_nt(kb.astype(BF16), k) * dec, 0.0)
              for kb, k, dec in zip(k_betas, ks, decays)]
    intras = [(_dot_nt(q, k) * dec).astype(BF16) for q, k, dec in zip(qs, ks, decays)]
    t_offs = _unit_lower_inverse(a_mats, row, col)
    e_gs = [jnp.exp(g_col) for g_col in g_cols]
    rhss = [jnp.concatenate([v_ref[bi, rs, hs].astype(F32) * b_col, kb * e_g], axis=1)
            for (bi, rs, _, hs), b_col, kb, e_g in zip(tiles, b_cols, k_betas, e_gs)]
    sols = [rhs + _dot(t.astype(BF16), rhs.astype(BF16)) for t, rhs in zip(t_offs, rhss)]
    q_decs = [(q.astype(F32) * e_g).astype(BF16) for q, e_g in zip(qs, e_gs)]
    g_lasts = [g_col[c - 1:c, :] for g_col in g_cols]
    k_decs = [(k.astype(F32) * jnp.exp(g_last - g_col)).astype(BF16)
              for k, g_last, g_col in zip(ks, g_lasts, g_cols)]

    n_ch = len(chains)
    for ci in range(tc // c):
        idx = [ci * n_ch + j for j in range(n_ch)]
        states = [state_ref[bi, h] for bi, h in chains]
        state_bs = [st.astype(BF16) for st in states]
        v_news = [sols[i][:, :DN_HEAD] - _dot(sols[i][:, DN_HEAD:].astype(BF16), sb)
                  for i, sb in zip(idx, state_bs)]
        inter = [_dot(q_decs[i], sb) for i, sb in zip(idx, state_bs)]
        v_new_bs = [vn.astype(BF16) for vn in v_news]
        outs = [o + _dot(intras[i], vb) for i, o, vb in zip(idx, inter, v_new_bs)]
        for j, i in enumerate(idx):
            bi, h = chains[j]
            state_ref[bi, h] = states[j] * jnp.exp(g_lasts[i]) + _dot_tn(k_decs[i], v_new_bs[j])
        for j, i in enumerate(idx):
            bi, rs, _, hs = tiles[i]
            o = outs[j]
            ms = jnp.mean(o * o, axis=-1, keepdims=True)
            o = o * lax.rsqrt(ms + EPS) * nw_ref[...]
            o_ref[bi, rs, hs] = (o * z_ref[bi, rs, hs].astype(F32)).astype(BF16)


def _split_dot_left(m_bf16, a):
    acc = None
    rem = a
    for _ in range(3):
        part = rem.astype(BF16)
        d = _dot(m_bf16, part)
        acc = d if acc is None else acc + d
        rem = rem - part.astype(F32)
    return acc


def _deltanet(qa, ka, va, zs, ab, alog_n, dtb_n, nw):
    B, S, _ = qa.shape
    tc = DN_TILE
    n_seq = DN_SEQS if B % DN_SEQS == 0 else 1
    tok = lambda w: pl.BlockSpec((n_seq, tc, w), lambda b, s: (b, s, 0))
    full = lambda a: pl.BlockSpec(a.shape, lambda b, s: (0,) * a.ndim)
    return pl.pallas_call(
        _deltanet_kernel,
        grid=(B // n_seq, S // tc),
        in_specs=[tok(HEADS_W), tok(HEADS_W), tok(HEADS_W), tok(HEADS_W), tok(LANES),
                  full(alog_n), full(dtb_n), full(nw)],
        out_specs=tok(HEADS_W),
        out_shape=jax.ShapeDtypeStruct((B, S, HEADS_W), BF16),
        scratch_shapes=[pltpu.VMEM((n_seq, DN_HEADS, DN_HEAD, DN_HEAD), F32)],
        compiler_params=pltpu.CompilerParams(
            dimension_semantics=("arbitrary", "arbitrary"), vmem_limit_bytes=VMEM_LIMIT_DN),
        name="deltanet",
    )(qa, ka, va, zs, ab, alog_n, dtb_n, nw)


def _diff_attn_kernel(qt_ref, k_ref, vt_ref, lq1_ref, lk1_ref, lq2_ref, lk2_ref, sw_ref, o_ref,
                      sa_ref, sb_ref, m_ref, acc_ref, *, lam_init):
    blk = ATT_BLK
    n_sub = ATT_QBLK // ATT_BLK
    iq = pl.program_id(2)
    hd = 2 * DF_HEAD

    q_maps = []
    for hh in range(ATT_HEADS_PER_STEP):
        qt = jnp.concatenate([qt_ref[0, i, hh * hd:(hh + 1) * hd, :] for i in range(n_sub)], axis=1)
        d_idx = lax.broadcasted_iota(jnp.int32, qt.shape, 0)
        zero = jnp.zeros_like(qt)
        q_maps.append(jnp.where(d_idx < DF_HEAD, qt, zero))
        q_maps.append(jnp.where(d_idx >= DF_HEAD, qt, zero))

    def scores(j, s_ref, lane0=0):
        rows = pl.ds(pl.multiple_of(j * blk, blk), blk)
        for hh in range(ATT_HEADS_PER_STEP):
            k_blk = k_ref[0, rows, hh * hd:(hh + 1) * hd]
            for m_i in range(2):
                c = 2 * hh + m_i
                s_ref[c, :, lane0:] = _dot(k_blk, q_maps[c][:, lane0:])

    def consume(j, s_ref, lane0=None):
        masked = lane0 is not None
        lane0 = lane0 or 0
        for hh in range(ATT_HEADS_PER_STEP):
            v_blk = vt_ref[0, j, hh * DV_AUG:(hh + 1) * DV_AUG, :]
            for m_i in range(2):
                c = 2 * hh + m_i
                s = s_ref[c, :, lane0:]
                if masked:
                    kpos = lax.broadcasted_iota(jnp.int32, s.shape, 0)
                    qpos = lax.broadcasted_iota(jnp.int32, s.shape, 1)
                    s = jnp.where(kpos <= qpos, s, -jnp.inf)
                m_old = m_ref[c, :, lane0:]
                m_new = jnp.maximum(m_old, jnp.max(s, axis=0, keepdims=True))
                alpha = jnp.exp2(m_old - m_new)
                p = jnp.exp2(s - m_new).astype(BF16)
                m_ref[c, :, lane0:] = m_new
                acc_ref[c, :, lane0:] = alpha * acc_ref[c, :, lane0:] + _dot(v_blk, p)

    m_ref[...] = jnp.full(m_ref.shape, -jnp.inf, F32)
    acc_ref[...] = jnp.zeros(acc_ref.shape, F32)
    scores(0, sa_ref)

    def pair(t, _):
        a = 2 * t
        scores(a + 1, sb_ref)
        consume(a, sa_ref)
        scores(a + 2, sa_ref)
        consume(a + 1, sb_ref)
        return 0

    lax.fori_loop(0, iq, pair, 0)

    scores(n_sub * iq + 1, sb_ref, lane0=blk)
    consume(n_sub * iq, sa_ref, lane0=0)
    consume(n_sub * iq + 1, sb_ref, lane0=blk)

    lam = (jnp.exp(jnp.sum(lq1_ref[...] * lk1_ref[...], axis=-1, keepdims=True))
           - jnp.exp(jnp.sum(lq2_ref[...] * lk2_ref[...], axis=-1, keepdims=True)) + lam_init)
    for hh in range(ATT_HEADS_PER_STEP):
        a1 = acc_ref[2 * hh]
        a2 = acc_ref[2 * hh + 1]
        inv1 = 1.0 / a1[DV:DV + 1]
        inv2 = lam / a2[DV:DV + 1]
        o = a1[:DV] * inv1 - a2[:DV] * inv2
        ms = jnp.mean(o * o, axis=0, keepdims=True)
        o = o * lax.rsqrt(ms + EPS) * sw_ref[...] * (1.0 - lam_init)
        o_ref[0, :, hh * DV:(hh + 1) * DV] = o.T.astype(BF16)


def _diff_attn(qt, kb, vt, lq1, lk1, lq2, lk2, sw, lam_init):
    B, S, _ = kb.shape
    nb = S // ATT_BLK
    hps = ATT_HEADS_PER_STEP
    chains = 2 * hps
    full = lambda a: pl.BlockSpec(a.shape, lambda b, g, i: (0,) * a.ndim)
    return pl.pallas_call(
        functools.partial(_diff_attn_kernel, lam_init=lam_init),
        grid=(B, DF_HEADS // hps, S // ATT_QBLK),
        in_specs=[pl.BlockSpec((1, ATT_QBLK // ATT_BLK, hps * DV, ATT_BLK),
                               lambda b, g, i: (b, i, g, 0)),
                  pl.BlockSpec((1, S, hps * DV), lambda b, g, i: (b, 0, g)),
                  pl.BlockSpec((1, nb, hps * DV_AUG, ATT_BLK), lambda b, g, i: (b, 0, g, 0)),
                  full(lq1), full(lk1), full(lq2), full(lk2), full(sw)],
        out_specs=pl.BlockSpec((1, ATT_QBLK, hps * DV), lambda b, g, i: (b, i, g)),
        out_shape=jax.ShapeDtypeStruct((B, S, HEADS_W), BF16),
        scratch_shapes=[pltpu.VMEM((chains, ATT_BLK, ATT_QBLK), F32),
                        pltpu.VMEM((chains, ATT_BLK, ATT_QBLK), F32),
                        pltpu.VMEM((chains, 1, ATT_QBLK), F32),
                        pltpu.VMEM((chains, DV_AUG, ATT_QBLK), F32)],
        compiler_params=pltpu.CompilerParams(
            dimension_semantics=("arbitrary", "arbitrary", "arbitrary"),
            vmem_limit_bytes=VMEM_LIMIT_ATT),
        name="diff_attn",
    )(qt, kb, vt, lq1, lk1, lq2, lk2, sw)


def _out_ffn_kernel(x_ref, oa_ref, ob_ref, wout_ref, n2w_ref, wup_ref, cw_ref, cb_ref, wdown_ref,
                    out_ref, carry_ref, ext_ref, act_ref):
    tm = x_ref.shape[1]
    ffn = wdown_ref.shape[0]
    fb = FFN_BLK

    @pl.when(pl.program_id(1) == 0)
    def _():
        carry_ref[...] = jnp.zeros(carry_ref.shape, F32)

    mix = jnp.concatenate([oa_ref[0], ob_ref[0]], axis=-1)
    h1 = x_ref[0] + _dot(mix, wout_ref[...])
    ms = jnp.mean(h1 * h1, axis=-1, keepdims=True)
    hn = (h1 * lax.rsqrt(ms + EPS) * n2w_ref[...]).astype(BF16)

    def conv_cols(lo, slot):
        cs = slice(lo, lo + fb)
        es = slice(slot * fb, (slot + 1) * fb)
        u = _dot(hn, wup_ref[:, cs])
        ext_ref[0:CONV_PAD, es] = carry_ref[:, cs]
        ext_ref[CONV_PAD:CONV_PAD + tm, es] = u
        carry_ref[:, cs] = u[tm - CONV_PAD:tm, :]
        y = cw_ref[FFN_CONV - 1:FFN_CONV, cs] * u + cb_ref[:, cs]
        for j in range(FFN_CONV - 1):
            off = CONV_PAD - (FFN_CONV - 1) + j
            y = y + cw_ref[j:j + 1, cs] * ext_ref[off:off + tm, es]
        return y

    for ci in range(ffn // fb):
        gate = conv_cols(ci * fb, 0)
        up = conv_cols(ffn + ci * fb, 1)
        act_ref[:, ci * fb:(ci + 1) * fb] = (_silu(gate) * up).astype(BF16)

    out_ref[0] = h1 + _dot(act_ref[...], wdown_ref[...])


def _out_ffn(x, oa, ob, wout, n2w, wup, cw, cb, wdown):
    B, S, D = x.shape
    tm = FFN_TOK_TILE
    ffn = wdown.shape[0]
    tok = lambda w: pl.BlockSpec((1, tm, w), lambda b, s: (b, s, 0))
    full = lambda a: pl.BlockSpec(a.shape, lambda b, s: (0,) * a.ndim,
                                  pipeline_mode=pl.Buffered(1))
    return pl.pallas_call(
        _out_ffn_kernel,
        grid=(B, S // tm),
        in_specs=[tok(D), tok(HEADS_W), tok(HEADS_W), full(wout), full(n2w), full(wup), full(cw),
                  full(cb), full(wdown)],
        out_specs=tok(D),
        out_shape=jax.ShapeDtypeStruct((B, S, D), F32),
        scratch_shapes=[pltpu.VMEM((CONV_PAD, 2 * ffn), F32),
                        pltpu.VMEM((tm + CONV_PAD, 2 * FFN_BLK), F32),
                        pltpu.VMEM((tm, ffn), BF16)],
        compiler_params=pltpu.CompilerParams(
            dimension_semantics=("arbitrary", "arbitrary"), vmem_limit_bytes=VMEM_LIMIT_FFN),
        name="out_ffn",
    )(x, oa, ob, wout, n2w, wup, cw, cb, wdown)


def _lane_place(v, offset):
    return jnp.zeros((1, LANES), F32).at[0, offset:offset + v.shape[0]].set(v.astype(F32))


def kernel(x, positions, norm1_w, w_in, dn_conv_w, dn_a_log, dn_dt_bias, dn_norm_w, df_q_norm_w,
           df_k_norm_w, df_lambda_q1, df_lambda_k1, df_lambda_q2, df_lambda_k2, df_subln_w, w_out,
           norm2_w, w_up, ffn_conv_w, ffn_conv_b, w_down):
    B, S, D = x.shape
    depth = w_in.shape[0]
    w = HEADS_W
    h = x
    pos3 = positions.reshape(B, 1, S)
    inv_freq = ROPE_THETA ** (-jnp.arange(0, ROPE_DIM, 2, dtype=F32) / ROPE_DIM)
    invf = jnp.broadcast_to(inv_freq[:, None], (ROPE_HALF, ATT_BLK))
    for l in range(depth):
        lam_init = 0.8 - 0.6 * math.exp(-0.3 * l)
        wi = w_in[l].astype(BF16)
        wqkv = wi[:, 0:3 * w]
        wz = wi[:, 3 * w:4 * w]
        n_gate = 2 * DN_HEADS
        wab = jnp.pad(wi[:, 4 * w:4 * w + n_gate], ((0, 0), (0, LANES - n_gate)))
        base = 4 * w + n_gate
        wfqt = wi[:, base:base + w].T
        wfkt = wi[:, base + w:base + 2 * w].T
        wfvt = wi[:, base + 2 * w:base + 3 * w].T
        qnw = jnp.broadcast_to(df_q_norm_w[l].astype(F32)[:, None], (DF_HEAD, ATT_BLK))
        knw = jnp.broadcast_to(df_k_norm_w[l].astype(F32)[:, None], (DF_HEAD, ATT_BLK))

        qa, ka, va, zs, ab, qt, kb, vt = _in_proj(
            h, pos3, norm1_w[l][None, :], wqkv, wz, wab, wfqt, wfkt, wfvt, dn_conv_w[l], qnw, knw,
            invf)

        o_a = _deltanet(qa, ka, va, zs, ab, _lane_place(dn_a_log[l], 0),
                        _lane_place(dn_dt_bias[l], 0), dn_norm_w[l][None, :].astype(F32))

        sw = jnp.broadcast_to(df_subln_w[l].astype(F32)[:, None], (2 * DF_HEAD, ATT_QBLK))
        o_b = _diff_attn(qt, kb, vt, df_lambda_q1[l][None, :], df_lambda_k1[l][None, :],
                         df_lambda_q2[l][None, :], df_lambda_k2[l][None, :], sw, lam_init)

        h = _out_ffn(h, o_a, o_b, w_out[l].astype(BF16), norm2_w[l][None, :],
                     w_up[l].astype(BF16), ffn_conv_w[l], ffn_conv_b[l][None, :],
                     w_down[l].astype(BF16))
    return h
```

```python
import functools
import math

import jax
import jax.numpy as jnp
from jax import lax
from jax.experimental import pallas as pl
from jax.experimental.pallas import tpu as pltpu

F32 = jnp.float32
BF16 = jnp.bfloat16

EPS = 1e-6
ROPE_THETA = 500000.0

DN_HEADS = 4
DN_HEAD = 128
DN_CONV = 4
DF_HEADS = 4
DF_HEAD = 64
ROPE_DIM = DF_HEAD // 4
ROPE_HALF = ROPE_DIM // 2
FFN_CONV = 3
HEADS_W = 512

LANES = 128
SUBLANES = 8
MXU_DIM = 256

TOK_TILE = 1024
FFN_TOK_TILE = 512
DN_TILE = 256
DN_CHUNK = 128
DN_SEQS = 4
INV_BASE = 16
ATT_BLK = 256
ATT_QBLK = 2 * ATT_BLK
FFN_BLK = 256
CONV_PAD = SUBLANES
ATT_HEADS_PER_STEP = 4
DV = 2 * DF_HEAD
BF16_SUBLANES = 2 * SUBLANES
DV_AUG = DV + BF16_SUBLANES
LOG2E = math.log2(math.e)

VMEM_LIMIT_PROJ = 48 * 1024 * 1024
VMEM_LIMIT_DN = 32 * 1024 * 1024
VMEM_LIMIT_ATT = 40 * 1024 * 1024
VMEM_LIMIT_FFN = 56 * 1024 * 1024


def _dot(a, b):
    return jnp.dot(a, b, preferred_element_type=F32)


def _dot_nt(a, b):
    return lax.dot_general(a, b, (((1,), (1,)), ((), ())), preferred_element_type=F32)


def _dot_tn(a, b):
    return lax.dot_general(a, b, (((0,), (0,)), ((), ())), preferred_element_type=F32)


def _silu(x):
    return x * jax.nn.sigmoid(x)


def _softplus(x):
    return jnp.maximum(x, 0.0) + jnp.log(1.0 + jnp.exp(-jnp.abs(x)))


def _qk_norm_rope(t, nw, cos, sin):
    n = t.shape[-1]
    t3 = t.reshape(2 * DF_HEADS, DF_HEAD, n)
    ms = jnp.mean(t3 * t3, axis=1, keepdims=True)
    t3 = t3 * lax.rsqrt(ms + EPS) * nw[None]
    x1 = t3[:, 0:ROPE_HALF, :]
    x2 = t3[:, ROPE_HALF:ROPE_DIM, :]
    c = cos[None]
    s = sin[None]
    out = jnp.concatenate([x1 * c - x2 * s, x2 * c + x1 * s, t3[:, ROPE_DIM:, :]], axis=1)
    return out.reshape(2 * DF_HEADS * DF_HEAD, n)


def _in_proj_kernel(x_ref, pos_ref, n1w_ref, wqkv_ref, wz_ref, wab_ref, wfqt_ref, wfkt_ref,
                    wfvt_ref, convw_ref, qnw_ref, knw_ref, invf_ref,
                    qa_ref, ka_ref, va_ref, z_ref, ab_ref, qt_ref, kb_ref, vt_ref,
                    ext_ref):
    tm = x_ref.shape[1]
    s_idx = pl.program_id(1)

    @pl.when(s_idx == 0)
    def _():
        ext_ref[0:CONV_PAD, :] = jnp.zeros((CONV_PAD, ext_ref.shape[1]), F32)

    @pl.when(s_idx > 0)
    def _():
        ext_ref[0:CONV_PAD, :] = ext_ref[tm:tm + CONV_PAD, :]

    x = x_ref[0]
    ms = jnp.mean(x * x, axis=-1, keepdims=True)
    xn = (x * lax.rsqrt(ms + EPS) * n1w_ref[...]).astype(BF16)

    outs = (qa_ref, ka_ref, va_ref)
    t_weights = (wfqt_ref, wfkt_ref, wfvt_ref)
    t_proj = []
    for seg in range(3):
        lo = seg * HEADS_W
        u = _dot(xn, wqkv_ref[:, lo:lo + HEADS_W])
        ext_ref[CONV_PAD:CONV_PAD + tm, lo:lo + HEADS_W] = u
        t_proj.append(_dot_nt(t_weights[seg][...], xn))
        y = convw_ref[DN_CONV - 1:DN_CONV, lo:lo + HEADS_W] * u
        for j in range(DN_CONV - 1):
            off = CONV_PAD - (DN_CONV - 1) + j
            y = y + convw_ref[j:j + 1, lo:lo + HEADS_W] * ext_ref[off:off + tm, lo:lo + HEADS_W]
        y = _silu(y)
        for h in range(DN_HEADS):
            blk = y[:, h * DN_HEAD:(h + 1) * DN_HEAD]
            if seg < 2:
                ss = jnp.sum(blk * blk, axis=-1, keepdims=True)
                blk = blk * lax.rsqrt(ss + EPS)
                if seg == 0:
                    blk = blk * (DN_HEAD ** -0.5)
            outs[seg][0, :, h * DN_HEAD:(h + 1) * DN_HEAD] = blk.astype(BF16)

    z_ref[0] = _silu(_dot(xn, wz_ref[...])).astype(BF16)
    ab_ref[0] = _dot(xn, wab_ref[...])

    n_blk = tm // ATT_BLK
    pos = pos_ref[0].astype(F32)
    qt, kt, vt = t_proj
    for i in range(n_blk):
        sl = slice(i * ATT_BLK, (i + 1) * ATT_BLK)
        ang = pos[:, sl] * invf_ref[...]
        cos = jnp.cos(ang)
        sin = jnp.sin(ang)
        q_i = _qk_norm_rope(qt[:, sl], qnw_ref[...], cos, sin) * (DF_HEAD ** -0.5 * LOG2E)
        k_i = _qk_norm_rope(kt[:, sl], knw_ref[...], cos, sin)
        qt_ref[0, i] = q_i.astype(BF16)
        kb_ref[0, sl, :] = k_i.T.astype(BF16)
        ones_rows = (lax.broadcasted_iota(jnp.int32, (BF16_SUBLANES, ATT_BLK), 0) == 0).astype(BF16)
        for h in range(DF_HEADS):
            vt_ref[0, i, h * DV_AUG:h * DV_AUG + DV, :] = vt[h * DV:(h + 1) * DV, sl].astype(BF16)
            vt_ref[0, i, h * DV_AUG + DV:(h + 1) * DV_AUG, :] = ones_rows


def _in_proj(x, pos3, n1w, wqkv, wz, wab, wfqt, wfkt, wfvt, convw, qnw, knw, invf):
    B, S, D = x.shape
    tm = TOK_TILE
    n_blk = tm // ATT_BLK
    grid = (B, S // tm)
    full = lambda a: pl.BlockSpec(a.shape, lambda b, s: (0,) * a.ndim)
    tok = lambda w: pl.BlockSpec((1, tm, w), lambda b, s: (b, s, 0))
    tpose = pl.BlockSpec((1, n_blk, HEADS_W, ATT_BLK), lambda b, s: (b, s, 0, 0))
    slab = jax.ShapeDtypeStruct((B, S, HEADS_W), BF16)
    slab_t = jax.ShapeDtypeStruct((B, S // ATT_BLK, HEADS_W, ATT_BLK), BF16)
    v_rows = DF_HEADS * DV_AUG
    tpose_v = pl.BlockSpec((1, n_blk, v_rows, ATT_BLK), lambda b, s: (b, s, 0, 0))
    slab_tv = jax.ShapeDtypeStruct((B, S // ATT_BLK, v_rows, ATT_BLK), BF16)
    return pl.pallas_call(
        _in_proj_kernel,
        grid=grid,
        in_specs=[tok(D), pl.BlockSpec((1, 1, tm), lambda b, s: (b, 0, s)), full(n1w), full(wqkv),
                  full(wz), full(wab), full(wfqt), full(wfkt), full(wfvt), full(convw), full(qnw),
                  full(knw), full(invf)],
        out_specs=[tok(HEADS_W), tok(HEADS_W), tok(HEADS_W), tok(HEADS_W), tok(LANES),
                   tpose, tok(HEADS_W), tpose_v],
        out_shape=[slab, slab, slab, slab, jax.ShapeDtypeStruct((B, S, LANES), F32),
                   slab_t, slab, slab_tv],
        scratch_shapes=[pltpu.VMEM((tm + CONV_PAD, 3 * HEADS_W), F32)],
        compiler_params=pltpu.CompilerParams(
            dimension_semantics=("arbitrary", "arbitrary"), vmem_limit_bytes=VMEM_LIMIT_PROJ),
        name="in_proj",
    )(x, pos3, n1w, wqkv, wz, wab, wfqt, wfkt, wfvt, convw, qnw, knw, invf)


def _unit_lower_inverse(mats, row, col):
    c = mats[0].shape[0]
    eye = (row == col).astype(F32)

    def same_block(size):
        return (row // size) == (col // size)

    base = same_block(INV_BASE)
    ps = [jnp.where(base, a, 0.0) for a in mats]
    ns = [eye - d for d in ps]
    width = 2
    while width < INV_BASE:
        pbs = [p.astype(BF16) for p in ps]
        ps = [_dot(pb, pb) for pb in pbs]
        ns = [n + _dot(n.astype(BF16), p.astype(BF16)) for n, p in zip(ns, ps)]
        width *= 2
    size = INV_BASE
    while size < c:
        join = same_block(2 * size) & jnp.logical_not(same_block(size))
        nbs = [n.astype(BF16) for n in ns]
        nes = [_dot(nb, jnp.where(join, a, 0.0).astype(BF16)) for nb, a in zip(nbs, mats)]
        ns = [n - _dot(ne.astype(BF16), nb) for n, ne, nb in zip(ns, nes, nbs)]
        size *= 2
    return [n - eye for n in ns]


def _deltanet_kernel(q_ref, k_ref, v_ref, z_ref, ab_ref, alog_ref, dtb_ref, nw_ref, o_ref, state_ref):
    tc = q_ref.shape[1]
    c = DN_CHUNK

    @pl.when(pl.program_id(1) == 0)
    def _():
        state_ref[...] = jnp.zeros(state_ref.shape, F32)

    n_seq = q_ref.shape[0]
    r_t = lax.broadcasted_iota(jnp.int32, (tc, tc), 0)
    c_t = lax.broadcasted_iota(jnp.int32, (tc, tc), 1)
    tril_bd = ((r_t >= c_t) & ((r_t // c) == (c_t // c))).astype(BF16)
    gcum_b, beta_b, gcum_t = [], [], []
    for bi in range(n_seq):
        ab = ab_ref[bi]
        g = -jnp.exp(alog_ref[...]) * _softplus(ab + dtb_ref[...])
        beta = jax.nn.sigmoid(ab)
        gcum = _split_dot_left(tril_bd, g)
        gcum_b.append(jnp.concatenate(
            [jnp.broadcast_to(gcum[:, h:h + 1], (tc, DN_HEAD)) for h in range(DN_HEADS)], axis=1))
        beta_b.append(jnp.concatenate(
            [jnp.broadcast_to(beta[:, DN_HEADS + h:DN_HEADS + h + 1], (tc, DN_HEAD))
             for h in range(DN_HEADS)], axis=1))
        gcum_t.append(gcum.T)

    row = lax.broadcasted_iota(jnp.int32, (c, c), 0)
    col = lax.broadcasted_iota(jnp.int32, (c, c), 1)
    lower = row >= col
    strict = row > col

    chains = [(bi, h) for bi in range(n_seq) for h in range(DN_HEADS)]
    tiles = [(bi, slice(ci * c, (ci + 1) * c), h, slice(h * DN_HEAD, (h + 1) * DN_HEAD))
             for ci in range(tc // c) for bi, h in chains]
    qs = [q_ref[bi, rs, hs] for bi, rs, _, hs in tiles]
    ks = [k_ref[bi, rs, hs] for bi, rs, _, hs in tiles]
    g_cols = [gcum_b[bi][rs, hs] for bi, rs, _, hs in tiles]
    b_cols = [beta_b[bi][rs, hs] for bi, rs, _, hs in tiles]
    decays = [jnp.where(lower, jnp.exp(jnp.minimum(g_col - gcum_t[bi][h:h + 1, rs], 0.0)), 0.0)
              for g_col, (bi, rs, h, _) in zip(g_cols, tiles)]
    k_betas = [k.astype(F32) * b_col for k, b_col in zip(ks, b_cols)]
    a_mats = [jnp.where(strict, _dot_nt(kb.astype(BF16), k) * dec, 0.0)
              for kb, k, dec in zip(k_betas, ks, decays)]
    intras = [(_dot_nt(q, k) * dec).astype(BF16) for q, k, dec in zip(qs, ks, decays)]
    t_offs = _unit_lower_inverse(a_mats, row, col)
    e_gs = [jnp.exp(g_col) for g_col in g_cols]
    rhss = [jnp.concatenate([v_ref[bi, rs, hs].astype(F32) * b_col, kb * e_g], axis=1)
            for (bi, rs, _, hs), b_col, kb, e_g in zip(tiles, b_cols, k_betas, e_gs)]
    sols = [rhs + _dot(t.astype(BF16), rhs.astype(BF16)) for t, rhs in zip(t_offs, rhss)]
    q_decs = [(q.astype(F32) * e_g).astype(BF16) for q, e_g in zip(qs, e_gs)]
    g_lasts = [g_col[c - 1:c, :] for g_col in g_cols]
    k_decs = [(k.astype(F32) * jnp.exp(g_last - g_col)).astype(BF16)
              for k, g_last, g_col in zip(ks, g_lasts, g_cols)]

    n_ch = len(chains)
    for ci in range(tc // c):
        idx = [ci * n_ch + j for j in range(n_ch)]
        states = [state_ref[bi, h] for bi, h in chains]
        state_bs = [st.astype(BF16) for st in states]
        v_news = [sols[i][:, :DN_HEAD] - _dot(sols[i][:, DN_HEAD:].astype(BF16), sb)
                  for i, sb in zip(idx, state_bs)]
        inter = [_dot(q_decs[i], sb) for i, sb in zip(idx, state_bs)]
        v_new_bs = [vn.astype(BF16) for vn in v_news]
        outs = [o + _dot(intras[i], vb) for i, o, vb in zip(idx, inter, v_new_bs)]
        for j, i in enumerate(idx):
            bi, h = chains[j]
            state_ref[bi, h] = states[j] * jnp.exp(g_lasts[i]) + _dot_tn(k_decs[i], v_new_bs[j])
        for j, i in enumerate(idx):
            bi, rs, _, hs = tiles[i]
            o = outs[j]
            ms = jnp.mean(o * o, axis=-1, keepdims=True)
            o = o * lax.rsqrt(ms + EPS) * nw_ref[...]
            o_ref[bi, rs, hs] = (o * z_ref[bi, rs, hs].astype(F32)).astype(BF16)


def _split_dot_left(m_bf16, a):
    acc = None
    rem = a
    for _ in range(3):
        part = rem.astype(BF16)
        d = _dot(m_bf16, part)
        acc = d if acc is None else acc + d
        rem = rem - part.astype(F32)
    return acc


def _deltanet(qa, ka, va, zs, ab, alog_n, dtb_n, nw):
    B, S, _ = qa.shape
    tc = DN_TILE
    n_seq = DN_SEQS if B % DN_SEQS == 0 else 1
    tok = lambda w: pl.BlockSpec((n_seq, tc, w), lambda b, s: (b, s, 0))
    full = lambda a: pl.BlockSpec(a.shape, lambda b, s: (0,) * a.ndim)
    return pl.pallas_call(
        _deltanet_kernel,
        grid=(B // n_seq, S // tc),
        in_specs=[tok(HEADS_W), tok(HEADS_W), tok(HEADS_W), tok(HEADS_W), tok(LANES),
                  full(alog_n), full(dtb_n), full(nw)],
        out_specs=tok(HEADS_W),
        out_shape=jax.ShapeDtypeStruct((B, S, HEADS_W), BF16),
        scratch_shapes=[pltpu.VMEM((n_seq, DN_HEADS, DN_HEAD, DN_HEAD), F32)],
        compiler_params=pltpu.CompilerParams(
            dimension_semantics=("arbitrary", "arbitrary"), vmem_limit_bytes=VMEM_LIMIT_DN),
        name="deltanet",
    )(qa, ka, va, zs, ab, alog_n, dtb_n, nw)


def _diff_attn_kernel(qt_ref, k_ref, vt_ref, lq1_ref, lk1_ref, lq2_ref, lk2_ref, sw_ref, o_ref,
                      pa_ref, pb_ref, ala_ref, alb_ref, m_ref, acc_ref, *, lam_init):
    blk = ATT_BLK
    n_sub = ATT_QBLK // ATT_BLK
    iq = pl.program_id(2)
    hd = 2 * DF_HEAD

    q_maps = []
    for hh in range(ATT_HEADS_PER_STEP):
        qt = jnp.concatenate([qt_ref[0, i, hh * hd:(hh + 1) * hd, :] for i in range(n_sub)], axis=1)
        d_idx = lax.broadcasted_iota(jnp.int32, qt.shape, 0)
        zero = jnp.zeros_like(qt)
        q_maps.append(jnp.where(d_idx < DF_HEAD, qt, zero))
        q_maps.append(jnp.where(d_idx >= DF_HEAD, qt, zero))

    def probs(j, p_ref, al_ref, mask_lane0=None):
        lane0 = mask_lane0 or 0
        rows = pl.ds(pl.multiple_of(j * blk, blk), blk)
        for hh in range(ATT_HEADS_PER_STEP):
            k_blk = k_ref[0, rows, hh * hd:(hh + 1) * hd]
            for m_i in range(2):
                c = 2 * hh + m_i
                s = _dot(k_blk, q_maps[c][:, lane0:])
                if mask_lane0 is not None:
                    kpos = lax.broadcasted_iota(jnp.int32, s.shape, 0)
                    qpos = lax.broadcasted_iota(jnp.int32, s.shape, 1)
                    s = jnp.where(kpos <= qpos, s, -jnp.inf)
                m_old = m_ref[c, :, lane0:]
                m_new = jnp.maximum(m_old, jnp.max(s, axis=0, keepdims=True))
                m_ref[c, :, lane0:] = m_new
                al_ref[c, :, lane0:] = jnp.exp2(m_old - m_new)
                p_ref[c, :, lane0:] = jnp.exp2(s - m_new).astype(BF16)

    def values(j, p_ref, al_ref, lane0=0):
        for hh in range(ATT_HEADS_PER_STEP):
            v_blk = vt_ref[0, j, hh * DV_AUG:(hh + 1) * DV_AUG, :]
            for m_i in range(2):
                c = 2 * hh + m_i
                acc_ref[c, :, lane0:] = (al_ref[c, :, lane0:] * acc_ref[c, :, lane0:]
                                         + _dot(v_blk, p_ref[c, :, lane0:]))

    m_ref[...] = jnp.full(m_ref.shape, -jnp.inf, F32)
    acc_ref[...] = jnp.zeros(acc_ref.shape, F32)
    diag0 = n_sub * iq
    bufs = ((pa_ref, ala_ref), (pb_ref, alb_ref))

    def drain(first_pending, blocks):
        pending = first_pending
        for j, par, lane0 in blocks:
            probs(j, *bufs[par], mask_lane0=lane0)
            if pending is not None:
                values(pending[0], *bufs[pending[1]], lane0=pending[2] or 0)
            pending = (j, par, lane0)
        values(pending[0], *bufs[pending[1]], lane0=pending[2] or 0)

    diag_blocks = [(diag0 + d, d % 2, d * blk) for d in range(n_sub)]

    @pl.when(iq == 0)
    def _():
        drain(None, diag_blocks)

    @pl.when(iq > 0)
    def _():
        probs(0, pa_ref, ala_ref)

        def pair(t, _):
            a = 2 * t
            probs(a + 1, pb_ref, alb_ref)
            values(a, pa_ref, ala_ref)
            probs(a + 2, pa_ref, ala_ref)
            values(a + 1, pb_ref, alb_ref)
            return 0

        lax.fori_loop(0, (n_sub // 2) * iq - 1, pair, 0)

        drain((diag0 - 2, 0, None), [(diag0 - 1, 1, None)] + diag_blocks)

    lam = (jnp.exp(jnp.sum(lq1_ref[...] * lk1_ref[...], axis=-1, keepdims=True))
           - jnp.exp(jnp.sum(lq2_ref[...] * lk2_ref[...], axis=-1, keepdims=True)) + lam_init)
    for hh in range(ATT_HEADS_PER_STEP):
        a1 = acc_ref[2 * hh]
        a2 = acc_ref[2 * hh + 1]
        inv1 = 1.0 / a1[DV:DV + 1]
        inv2 = lam / a2[DV:DV + 1]
        o = a1[:DV] * inv1 - a2[:DV] * inv2
        ms = jnp.mean(o * o, axis=0, keepdims=True)
        o = o * lax.rsqrt(ms + EPS) * sw_ref[...] * (1.0 - lam_init)
        o_ref[0, :, hh * DV:(hh + 1) * DV] = o.T.astype(BF16)


def _diff_attn(qt, kb, vt, lq1, lk1, lq2, lk2, sw, lam_init):
    B, S, _ = kb.shape
    nb = S // ATT_BLK
    hps = ATT_HEADS_PER_STEP
    chains = 2 * hps
    full = lambda a: pl.BlockSpec(a.shape, lambda b, g, i: (0,) * a.ndim)
    return pl.pallas_call(
        functools.partial(_diff_attn_kernel, lam_init=lam_init),
        grid=(B, DF_HEADS // hps, S // ATT_QBLK),
        in_specs=[pl.BlockSpec((1, ATT_QBLK // ATT_BLK, hps * DV, ATT_BLK),
                               lambda b, g, i: (b, i, g, 0)),
                  pl.BlockSpec((1, S, hps * DV), lambda b, g, i: (b, 0, g)),
                  pl.BlockSpec((1, nb, hps * DV_AUG, ATT_BLK), lambda b, g, i: (b, 0, g, 0)),
                  full(lq1), full(lk1), full(lq2), full(lk2), full(sw)],
        out_specs=pl.BlockSpec((1, ATT_QBLK, hps * DV), lambda b, g, i: (b, i, g)),
        out_shape=jax.ShapeDtypeStruct((B, S, HEADS_W), BF16),
        scratch_shapes=[pltpu.VMEM((chains, ATT_BLK, ATT_QBLK), BF16),
                        pltpu.VMEM((chains, ATT_BLK, ATT_QBLK), BF16),
                        pltpu.VMEM((chains, 1, ATT_QBLK), F32),
                        pltpu.VMEM((chains, 1, ATT_QBLK), F32),
                        pltpu.VMEM((chains, 1, ATT_QBLK), F32),
                        pltpu.VMEM((chains, DV_AUG, ATT_QBLK), F32)],
        compiler_params=pltpu.CompilerParams(
            dimension_semantics=("arbitrary", "arbitrary", "arbitrary"),
            vmem_limit_bytes=VMEM_LIMIT_ATT),
        name="diff_attn",
    )(qt, kb, vt, lq1, lk1, lq2, lk2, sw)


def _out_ffn_kernel(x_ref, oa_ref, ob_ref, wout_ref, n2w_ref, wup_ref, cw_ref, cb_ref, wdown_ref,
                    out_ref, carry_ref, ext_ref, act_ref):
    tm = x_ref.shape[1]
    ffn = wdown_ref.shape[0]
    fb = FFN_BLK

    @pl.when(pl.program_id(1) == 0)
    def _():
        carry_ref[...] = jnp.zeros(carry_ref.shape, F32)

    mix = jnp.concatenate([oa_ref[0], ob_ref[0]], axis=-1)
    h1 = x_ref[0] + _dot(mix, wout_ref[...])
    ms = jnp.mean(h1 * h1, axis=-1, keepdims=True)
    hn = (h1 * lax.rsqrt(ms + EPS) * n2w_ref[...]).astype(BF16)

    def conv_cols(lo, slot):
        cs = slice(lo, lo + fb)
        es = slice(slot * fb, (slot + 1) * fb)
        u = _dot(hn, wup_ref[:, cs])
        ext_ref[0:CONV_PAD, es] = carry_ref[:, cs]
        ext_ref[CONV_PAD:CONV_PAD + tm, es] = u
        carry_ref[:, cs] = u[tm - CONV_PAD:tm, :]
        y = cw_ref[FFN_CONV - 1:FFN_CONV, cs] * u + cb_ref[:, cs]
        for j in range(FFN_CONV - 1):
            off = CONV_PAD - (FFN_CONV - 1) + j
            y = y + cw_ref[j:j + 1, cs] * ext_ref[off:off + tm, es]
        return y

    for ci in range(ffn // fb):
        gate = conv_cols(ci * fb, 0)
        up = conv_cols(ffn + ci * fb, 1)
        act_ref[:, ci * fb:(ci + 1) * fb] = (_silu(gate) * up).astype(BF16)

    out_ref[0] = h1 + _dot(act_ref[...], wdown_ref[...])


def _out_ffn(x, oa, ob, wout, n2w, wup, cw, cb, wdown):
    B, S, D = x.shape
    tm = FFN_TOK_TILE
    ffn = wdown.shape[0]
    tok = lambda w: pl.BlockSpec((1, tm, w), lambda b, s: (b, s, 0))
    full = lambda a: pl.BlockSpec(a.shape, lambda b, s: (0,) * a.ndim,
                                  pipeline_mode=pl.Buffered(1))
    return pl.pallas_call(
        _out_ffn_kernel,
        grid=(B, S // tm),
        in_specs=[tok(D), tok(HEADS_W), tok(HEADS_W), full(wout), full(n2w), full(wup), full(cw),
                  full(cb), full(wdown)],
        out_specs=tok(D),
        out_shape=jax.ShapeDtypeStruct((B, S, D), F32),
        scratch_shapes=[pltpu.VMEM((CONV_PAD, 2 * ffn), F32),
                        pltpu.VMEM((tm + CONV_PAD, 2 * FFN_BLK), F32),
                        pltpu.VMEM((tm, ffn), BF16)],
        compiler_params=pltpu.CompilerParams(
            dimension_semantics=("arbitrary", "arbitrary"), vmem_limit_bytes=VMEM_LIMIT_FFN),
        name="out_ffn",
    )(x, oa, ob, wout, n2w, wup, cw, cb, wdown)


def _lane_place(v, offset):
    return jnp.zeros((1, LANES), F32).at[0, offset:offset + v.shape[0]].set(v.astype(F32))


def kernel(x, positions, norm1_w, w_in, dn_conv_w, dn_a_log, dn_dt_bias, dn_norm_w, df_q_norm_w,
           df_k_norm_w, df_lambda_q1, df_lambda_k1, df_lambda_q2, df_lambda_k2, df_subln_w, w_out,
           norm2_w, w_up, ffn_conv_w, ffn_conv_b, w_down):
    B, S, D = x.shape
    depth = w_in.shape[0]
    w = HEADS_W
    h = x
    pos3 = positions.reshape(B, 1, S)
    inv_freq = ROPE_THETA ** (-jnp.arange(0, ROPE_DIM, 2, dtype=F32) / ROPE_DIM)
    invf = jnp.broadcast_to(inv_freq[:, None], (ROPE_HALF, ATT_BLK))
    for l in range(depth):
        lam_init = 0.8 - 0.6 * math.exp(-0.3 * l)
        wi = w_in[l].astype(BF16)
        wqkv = wi[:, 0:3 * w]
        wz = wi[:, 3 * w:4 * w]
        n_gate = 2 * DN_HEADS
        wab = jnp.pad(wi[:, 4 * w:4 * w + n_gate], ((0, 0), (0, LANES - n_gate)))
        base = 4 * w + n_gate
        wfqt = wi[:, base:base + w].T
        wfkt = wi[:, base + w:base + 2 * w].T
        wfvt = wi[:, base + 2 * w:base + 3 * w].T
        qnw = jnp.broadcast_to(df_q_norm_w[l].astype(F32)[:, None], (DF_HEAD, ATT_BLK))
        knw = jnp.broadcast_to(df_k_norm_w[l].astype(F32)[:, None], (DF_HEAD, ATT_BLK))

        qa, ka, va, zs, ab, qt, kb, vt = _in_proj(
            h, pos3, norm1_w[l][None, :], wqkv, wz, wab, wfqt, wfkt, wfvt, dn_conv_w[l], qnw, knw,
            invf)

        o_a = _deltanet(qa, ka, va, zs, ab, _lane_place(dn_a_log[l], 0),
                        _lane_place(dn_dt_bias[l], 0), dn_norm_w[l][None, :].astype(F32))

        sw = jnp.broadcast_to(df_subln_w[l].astype(F32)[:, None], (2 * DF_HEAD, ATT_QBLK))
        o_b = _diff_attn(qt, kb, vt, df_lambda_q1[l][None, :], df_lambda_k1[l][None, :],
                         df_lambda_q2[l][None, :], df_lambda_k2[l][None, :], sw, lam_init)

        h = _out_ffn(h, o_a, o_b, w_out[l].astype(BF16), norm2_w[l][None, :],
                     w_up[l].astype(BF16), ffn_conv_w[l], ffn_conv_b[l][None, :],
                     w_down[l].astype(BF16))
    return h
```

```python
import functools
import math

import jax
import jax.numpy as jnp
from jax import lax
from jax.experimental import pallas as pl
from jax.experimental.pallas import tpu as pltpu

F32 = jnp.float32
BF16 = jnp.bfloat16

EPS = 1e-6
ROPE_THETA = 500000.0

DN_HEADS = 4
DN_HEAD = 128
DN_CONV = 4
DF_HEADS = 4
DF_HEAD = 64
ROPE_DIM = DF_HEAD // 4
ROPE_HALF = ROPE_DIM // 2
FFN_CONV = 3
HEADS_W = 512

LANES = 128
SUBLANES = 8
MXU_DIM = 256

TOK_TILE = 1024
FFN_TOK_TILE = 512
DN_TILE = 256
DN_CHUNK = 128
DN_SEQS = 4
INV_BASE = 16
ATT_BLK = 256
ATT_QBLK = 2 * ATT_BLK
FFN_BLK = 256
CONV_PAD = SUBLANES
ATT_HEADS_PER_STEP = 4
DV = 2 * DF_HEAD
BF16_SUBLANES = 2 * SUBLANES
DV_AUG = DV + BF16_SUBLANES
LOG2E = math.log2(math.e)

VMEM_LIMIT_PROJ = 48 * 1024 * 1024
VMEM_LIMIT_DN = 32 * 1024 * 1024
VMEM_LIMIT_ATT = 40 * 1024 * 1024
VMEM_LIMIT_FFN = 56 * 1024 * 1024


def _dot(a, b):
    return jnp.dot(a, b, preferred_element_type=F32)


def _dot_nt(a, b):
    return lax.dot_general(a, b, (((1,), (1,)), ((), ())), preferred_element_type=F32)


def _dot_tn(a, b):
    return lax.dot_general(a, b, (((0,), (0,)), ((), ())), preferred_element_type=F32)


def _silu(x):
    return x * jax.nn.sigmoid(x)


def _softplus(x):
    return jnp.maximum(x, 0.0) + jnp.log(1.0 + jnp.exp(-jnp.abs(x)))


def _qk_norm_rope(t, nw, cos, sin):
    n = t.shape[-1]
    t3 = t.reshape(2 * DF_HEADS, DF_HEAD, n)
    ms = jnp.mean(t3 * t3, axis=1, keepdims=True)
    t3 = t3 * lax.rsqrt(ms + EPS) * nw[None]
    x1 = t3[:, 0:ROPE_HALF, :]
    x2 = t3[:, ROPE_HALF:ROPE_DIM, :]
    c = cos[None]
    s = sin[None]
    out = jnp.concatenate([x1 * c - x2 * s, x2 * c + x1 * s, t3[:, ROPE_DIM:, :]], axis=1)
    return out.reshape(2 * DF_HEADS * DF_HEAD, n)


def _in_proj_kernel(x_ref, pos_ref, n1w_ref, wqkv_ref, wz_ref, wab_ref, wfqt_ref, wfkt_ref,
                    wfvt_ref, convw_ref, qnw_ref, knw_ref, invf_ref,
                    qa_ref, ka_ref, va_ref, z_ref, ab_ref, qt_ref, kb_ref, vt_ref,
                    ext_ref):
    tm = x_ref.shape[1]
    s_idx = pl.program_id(1)

    @pl.when(s_idx == 0)
    def _():
        ext_ref[0:CONV_PAD, :] = jnp.zeros((CONV_PAD, ext_ref.shape[1]), F32)

    @pl.when(s_idx > 0)
    def _():
        ext_ref[0:CONV_PAD, :] = ext_ref[tm:tm + CONV_PAD, :]

    x = x_ref[0]
    ms = jnp.mean(x * x, axis=-1, keepdims=True)
    xn = (x * lax.rsqrt(ms + EPS) * n1w_ref[...]).astype(BF16)

    z_ref[0] = _silu(_dot(xn, wz_ref[...])).astype(BF16)
    ab_ref[0] = _dot(xn, wab_ref[...])
    outs = (qa_ref, ka_ref, va_ref)
    t_weights = (wfqt_ref, wfkt_ref, wfvt_ref)
    t_proj = []
    for seg in range(3):
        lo = seg * HEADS_W
        u = _dot(xn, wqkv_ref[:, lo:lo + HEADS_W])
        ext_ref[CONV_PAD:CONV_PAD + tm, lo:lo + HEADS_W] = u
        t_proj.append(_dot_nt(t_weights[seg][...], xn))
        y = convw_ref[DN_CONV - 1:DN_CONV, lo:lo + HEADS_W] * u
        for j in range(DN_CONV - 1):
            off = CONV_PAD - (DN_CONV - 1) + j
            y = y + convw_ref[j:j + 1, lo:lo + HEADS_W] * ext_ref[off:off + tm, lo:lo + HEADS_W]
        y = _silu(y)
        for h in range(DN_HEADS):
            blk = y[:, h * DN_HEAD:(h + 1) * DN_HEAD]
            if seg < 2:
                ss = jnp.sum(blk * blk, axis=-1, keepdims=True)
                blk = blk * lax.rsqrt(ss + EPS)
                if seg == 0:
                    blk = blk * (DN_HEAD ** -0.5)
            outs[seg][0, :, h * DN_HEAD:(h + 1) * DN_HEAD] = blk.astype(BF16)

    n_blk = tm // ATT_BLK
    pos = pos_ref[0].astype(F32)
    qt, kt, vt = t_proj
    for i in range(n_blk):
        sl = slice(i * ATT_BLK, (i + 1) * ATT_BLK)
        ang = pos[:, sl] * invf_ref[...]
        cos = jnp.cos(ang)
        sin = jnp.sin(ang)
        q_i = _qk_norm_rope(qt[:, sl], qnw_ref[...], cos, sin) * (DF_HEAD ** -0.5 * LOG2E)
        k_i = _qk_norm_rope(kt[:, sl], knw_ref[...], cos, sin)
        qt_ref[0, i] = q_i.astype(BF16)
        kb_ref[0, sl, :] = k_i.T.astype(BF16)
        ones_rows = (lax.broadcasted_iota(jnp.int32, (BF16_SUBLANES, ATT_BLK), 0) == 0).astype(BF16)
        for h in range(DF_HEADS):
            vt_ref[0, i, h * DV_AUG:h * DV_AUG + DV, :] = vt[h * DV:(h + 1) * DV, sl].astype(BF16)
            vt_ref[0, i, h * DV_AUG + DV:(h + 1) * DV_AUG, :] = ones_rows


def _in_proj(x, pos3, n1w, wqkv, wz, wab, wfqt, wfkt, wfvt, convw, qnw, knw, invf):
    B, S, D = x.shape
    tm = TOK_TILE
    n_blk = tm // ATT_BLK
    grid = (B, S // tm)
    full = lambda a: pl.BlockSpec(a.shape, lambda b, s: (0,) * a.ndim)
    tok = lambda w: pl.BlockSpec((1, tm, w), lambda b, s: (b, s, 0))
    tpose = pl.BlockSpec((1, n_blk, HEADS_W, ATT_BLK), lambda b, s: (b, s, 0, 0))
    slab = jax.ShapeDtypeStruct((B, S, HEADS_W), BF16)
    slab_t = jax.ShapeDtypeStruct((B, S // ATT_BLK, HEADS_W, ATT_BLK), BF16)
    v_rows = DF_HEADS * DV_AUG
    tpose_v = pl.BlockSpec((1, n_blk, v_rows, ATT_BLK), lambda b, s: (b, s, 0, 0))
    slab_tv = jax.ShapeDtypeStruct((B, S // ATT_BLK, v_rows, ATT_BLK), BF16)
    return pl.pallas_call(
        _in_proj_kernel,
        grid=grid,
        in_specs=[tok(D), pl.BlockSpec((1, 1, tm), lambda b, s: (b, 0, s)), full(n1w), full(wqkv),
                  full(wz), full(wab), full(wfqt), full(wfkt), full(wfvt), full(convw), full(qnw),
                  full(knw), full(invf)],
        out_specs=[tok(HEADS_W), tok(HEADS_W), tok(HEADS_W), tok(HEADS_W), tok(LANES),
                   tpose, tok(HEADS_W), tpose_v],
        out_shape=[slab, slab, slab, slab, jax.ShapeDtypeStruct((B, S, LANES), F32),
                   slab_t, slab, slab_tv],
        scratch_shapes=[pltpu.VMEM((tm + CONV_PAD, 3 * HEADS_W), F32)],
        compiler_params=pltpu.CompilerParams(
            dimension_semantics=("arbitrary", "arbitrary"), vmem_limit_bytes=VMEM_LIMIT_PROJ),
        name="in_proj",
    )(x, pos3, n1w, wqkv, wz, wab, wfqt, wfkt, wfvt, convw, qnw, knw, invf)


def _unit_lower_inverse(mats, row, col):
    c = mats[0].shape[0]
    eye = (row == col).astype(F32)

    def same_block(size):
        return (row // size) == (col // size)

    base = same_block(INV_BASE)
    ps = [jnp.where(base, a, 0.0) for a in mats]
    ns = [eye - d for d in ps]
    width = 2
    while width < INV_BASE:
        pbs = [p.astype(BF16) for p in ps]
        ps = [_dot(pb, pb) for pb in pbs]
        ns = [n + _dot(n.astype(BF16), p.astype(BF16)) for n, p in zip(ns, ps)]
        width *= 2
    size = INV_BASE
    while size < c:
        join = same_block(2 * size) & jnp.logical_not(same_block(size))
        nbs = [n.astype(BF16) for n in ns]
        nes = [_dot(nb, jnp.where(join, a, 0.0).astype(BF16)) for nb, a in zip(nbs, mats)]
        ns = [n - _dot(ne.astype(BF16), nb) for n, ne, nb in zip(ns, nes, nbs)]
        size *= 2
    return [n - eye for n in ns]


def _deltanet_kernel(q_ref, k_ref, v_ref, z_ref, ab_ref, alog_ref, dtb_ref, nw_ref, o_ref, state_ref):
    tc = q_ref.shape[1]
    c = DN_CHUNK

    @pl.when(pl.program_id(1) == 0)
    def _():
        state_ref[...] = jnp.zeros(state_ref.shape, F32)

    n_seq = q_ref.shape[0]
    r_t = lax.broadcasted_iota(jnp.int32, (tc, tc), 0)
    c_t = lax.broadcasted_iota(jnp.int32, (tc, tc), 1)
    tril_bd = ((r_t >= c_t) & ((r_t // c) == (c_t // c))).astype(BF16)
    gcum_b, beta_b, gcum_t = [], [], []
    for bi in range(n_seq):
        ab = ab_ref[bi]
        g = -jnp.exp(alog_ref[...]) * _softplus(ab + dtb_ref[...])
        beta = jax.nn.sigmoid(ab)
        gcum = _split_dot_left(tril_bd, g)
        gcum_b.append(jnp.concatenate(
            [jnp.broadcast_to(gcum[:, h:h + 1], (tc, DN_HEAD)) for h in range(DN_HEADS)], axis=1))
        beta_b.append(jnp.concatenate(
            [jnp.broadcast_to(beta[:, DN_HEADS + h:DN_HEADS + h + 1], (tc, DN_HEAD))
             for h in range(DN_HEADS)], axis=1))
        gcum_t.append(gcum.T)

    row = lax.broadcasted_iota(jnp.int32, (c, c), 0)
    col = lax.broadcasted_iota(jnp.int32, (c, c), 1)
    lower = row >= col
    strict = row > col

    chains = [(bi, h) for bi in range(n_seq) for h in range(DN_HEADS)]
    tiles = [(bi, slice(ci * c, (ci + 1) * c), h, slice(h * DN_HEAD, (h + 1) * DN_HEAD))
             for ci in range(tc // c) for bi, h in chains]
    qs = [q_ref[bi, rs, hs] for bi, rs, _, hs in tiles]
    ks = [k_ref[bi, rs, hs] for bi, rs, _, hs in tiles]
    g_cols = [gcum_b[bi][rs, hs] for bi, rs, _, hs in tiles]
    b_cols = [beta_b[bi][rs, hs] for bi, rs, _, hs in tiles]
    decays = [jnp.where(lower, jnp.exp(jnp.minimum(g_col - gcum_t[bi][h:h + 1, rs], 0.0)), 0.0)
              for g_col, (bi, rs, h, _) in zip(g_cols, tiles)]
    k_betas = [k.astype(F32) * b_col for k, b_col in zip(ks, b_cols)]
    a_mats = [jnp.where(strict, _dot_nt(kb.astype(BF16), k) * dec, 0.0)
              for kb, k, dec in zip(k_betas, ks, decays)]
    intras = [(_dot_nt(q, k) * dec).astype(BF16) for q, k, dec in zip(qs, ks, decays)]
    t_offs = _unit_lower_inverse(a_mats, row, col)
    e_gs = [jnp.exp(g_col) for g_col in g_cols]
    rhss = [jnp.concatenate([v_ref[bi, rs, hs].astype(F32) * b_col, kb * e_g], axis=1)
            for (bi, rs, _, hs), b_col, kb, e_g in zip(tiles, b_cols, k_betas, e_gs)]
    sols = [rhs + _dot(t.astype(BF16), rhs.astype(BF16)) for t, rhs in zip(t_offs, rhss)]
    q_decs = [(q.astype(F32) * e_g).astype(BF16) for q, e_g in zip(qs, e_gs)]
    g_lasts = [g_col[c - 1:c, :] for g_col in g_cols]
    k_decs = [(k.astype(F32) * jnp.exp(g_last - g_col)).astype(BF16)
              for k, g_last, g_col in zip(ks, g_lasts, g_cols)]

    n_ch = len(chains)
    for ci in range(tc // c):
        idx = [ci * n_ch + j for j in range(n_ch)]
        states = [state_ref[bi, h] for bi, h in chains]
        state_bs = [st.astype(BF16) for st in states]
        v_news = [sols[i][:, :DN_HEAD] - _dot(sols[i][:, DN_HEAD:].astype(BF16), sb)
                  for i, sb in zip(idx, state_bs)]
        inter = [_dot(q_decs[i], sb) for i, sb in zip(idx, state_bs)]
        v_new_bs = [vn.astype(BF16) for vn in v_news]
        outs = [o + _dot(intras[i], vb) for i, o, vb in zip(idx, inter, v_new_bs)]
        for j, i in enumerate(idx):
            bi, h = chains[j]
            state_ref[bi, h] = states[j] * jnp.exp(g_lasts[i]) + _dot_tn(k_decs[i], v_new_bs[j])
        for j, i in enumerate(idx):
            bi, rs, _, hs = tiles[i]
            o = outs[j]
            ms = jnp.mean(o * o, axis=-1, keepdims=True)
            o = o * lax.rsqrt(ms + EPS) * nw_ref[...]
            o_ref[bi, rs, hs] = (o * z_ref[bi, rs, hs].astype(F32)).astype(BF16)


def _split_dot_left(m_bf16, a):
    acc = None
    rem = a
    for _ in range(3):
        part = rem.astype(BF16)
        d = _dot(m_bf16, part)
        acc = d if acc is None else acc + d
        rem = rem - part.astype(F32)
    return acc


def _deltanet(qa, ka, va, zs, ab, alog_n, dtb_n, nw):
    B, S, _ = qa.shape
    tc = DN_TILE
    n_seq = DN_SEQS if B % DN_SEQS == 0 else 1
    tok = lambda w: pl.BlockSpec((n_seq, tc, w), lambda b, s: (b, s, 0))
    full = lambda a: pl.BlockSpec(a.shape, lambda b, s: (0,) * a.ndim)
    return pl.pallas_call(
        _deltanet_kernel,
        grid=(B // n_seq, S // tc),
        in_specs=[tok(HEADS_W), tok(HEADS_W), tok(HEADS_W), tok(HEADS_W), tok(LANES),
                  full(alog_n), full(dtb_n), full(nw)],
        out_specs=tok(HEADS_W),
        out_shape=jax.ShapeDtypeStruct((B, S, HEADS_W), BF16),
        scratch_shapes=[pltpu.VMEM((n_seq, DN_HEADS, DN_HEAD, DN_HEAD), F32)],
        compiler_params=pltpu.CompilerParams(
            dimension_semantics=("arbitrary", "arbitrary"), vmem_limit_bytes=VMEM_LIMIT_DN),
        name="deltanet",
    )(qa, ka, va, zs, ab, alog_n, dtb_n, nw)


def _diff_attn_kernel(qt_ref, k_ref, vt_ref, lq1_ref, lk1_ref, lq2_ref, lk2_ref, sw_ref, o_ref,
                      pa_ref, pb_ref, ala_ref, alb_ref, m_ref, acc_ref, *, lam_init):
    blk = ATT_BLK
    n_sub = ATT_QBLK // ATT_BLK
    iq = pl.program_id(2)
    hd = 2 * DF_HEAD

    q_maps = []
    for hh in range(ATT_HEADS_PER_STEP):
        qt = jnp.concatenate([qt_ref[0, i, hh * hd:(hh + 1) * hd, :] for i in range(n_sub)], axis=1)
        d_idx = lax.broadcasted_iota(jnp.int32, qt.shape, 0)
        zero = jnp.zeros_like(qt)
        q_maps.append(jnp.where(d_idx < DF_HEAD, qt, zero))
        q_maps.append(jnp.where(d_idx >= DF_HEAD, qt, zero))

    def probs(j, p_ref, al_ref, mask_lane0=None):
        lane0 = mask_lane0 or 0
        rows = pl.ds(pl.multiple_of(j * blk, blk), blk)
        for hh in range(ATT_HEADS_PER_STEP):
            k_blk = k_ref[0, rows, hh * hd:(hh + 1) * hd]
            for m_i in range(2):
                c = 2 * hh + m_i
                s = _dot(k_blk, q_maps[c][:, lane0:])
                if mask_lane0 is not None:
                    kpos = lax.broadcasted_iota(jnp.int32, s.shape, 0)
                    qpos = lax.broadcasted_iota(jnp.int32, s.shape, 1)
                    s = jnp.where(kpos <= qpos, s, -jnp.inf)
                s = s.astype(BF16)
                m_old = m_ref[c, :, lane0:]
                m_new = jnp.maximum(m_old, jnp.max(s, axis=0, keepdims=True).astype(F32))
                m_ref[c, :, lane0:] = m_new
                al_ref[c, :, lane0:] = jnp.exp2(m_old - m_new)
                p_ref[c, :, lane0:] = jnp.exp2(s - m_new.astype(BF16))

    def values(j, p_ref, al_ref, lane0=0):
        for hh in range(ATT_HEADS_PER_STEP):
            v_blk = vt_ref[0, j, hh * DV_AUG:(hh + 1) * DV_AUG, :]
            for m_i in range(2):
                c = 2 * hh + m_i
                acc_ref[c, :, lane0:] = (al_ref[c, :, lane0:] * acc_ref[c, :, lane0:]
                                         + _dot(v_blk, p_ref[c, :, lane0:]))

    m_ref[...] = jnp.full(m_ref.shape, -jnp.inf, F32)
    acc_ref[...] = jnp.zeros(acc_ref.shape, F32)
    diag0 = n_sub * iq
    bufs = ((pa_ref, ala_ref), (pb_ref, alb_ref))

    def drain(first_pending, blocks):
        pending = first_pending
        for j, par, lane0 in blocks:
            probs(j, *bufs[par], mask_lane0=lane0)
            if pending is not None:
                values(pending[0], *bufs[pending[1]], lane0=pending[2] or 0)
            pending = (j, par, lane0)
        values(pending[0], *bufs[pending[1]], lane0=pending[2] or 0)

    diag_blocks = [(diag0 + d, d % 2, d * blk) for d in range(n_sub)]

    @pl.when(iq == 0)
    def _():
        drain(None, diag_blocks)

    @pl.when(iq > 0)
    def _():
        probs(0, pa_ref, ala_ref)

        def pair(t, _):
            a = 2 * t
            probs(a + 1, pb_ref, alb_ref)
            values(a, pa_ref, ala_ref)
            probs(a + 2, pa_ref, ala_ref)
            values(a + 1, pb_ref, alb_ref)
            return 0

        lax.fori_loop(0, (n_sub // 2) * iq - 1, pair, 0)

        drain((diag0 - 2, 0, None), [(diag0 - 1, 1, None)] + diag_blocks)

    lam = (jnp.exp(jnp.sum(lq1_ref[...] * lk1_ref[...], axis=-1, keepdims=True))
           - jnp.exp(jnp.sum(lq2_ref[...] * lk2_ref[...], axis=-1, keepdims=True)) + lam_init)
    for hh in range(ATT_HEADS_PER_STEP):
        a1 = acc_ref[2 * hh]
        a2 = acc_ref[2 * hh + 1]
        inv1 = 1.0 / a1[DV:DV + 1]
        inv2 = lam / a2[DV:DV + 1]
        o = a1[:DV] * inv1 - a2[:DV] * inv2
        ms = jnp.mean(o * o, axis=0, keepdims=True)
        o = o * lax.rsqrt(ms + EPS) * sw_ref[...] * (1.0 - lam_init)
        o_ref[0, :, hh * DV:(hh + 1) * DV] = o.T.astype(BF16)


def _diff_attn(qt, kb, vt, lq1, lk1, lq2, lk2, sw, lam_init):
    B, S, _ = kb.shape
    nb = S // ATT_BLK
    hps = ATT_HEADS_PER_STEP
    chains = 2 * hps
    full = lambda a: pl.BlockSpec(a.shape, lambda b, g, i: (0,) * a.ndim)
    return pl.pallas_call(
        functools.partial(_diff_attn_kernel, lam_init=lam_init),
        grid=(B, DF_HEADS // hps, S // ATT_QBLK),
        in_specs=[pl.BlockSpec((1, ATT_QBLK // ATT_BLK, hps * DV, ATT_BLK),
                               lambda b, g, i: (b, i, g, 0)),
                  pl.BlockSpec((1, S, hps * DV), lambda b, g, i: (b, 0, g)),
                  pl.BlockSpec((1, nb, hps * DV_AUG, ATT_BLK), lambda b, g, i: (b, 0, g, 0)),
                  full(lq1), full(lk1), full(lq2), full(lk2), full(sw)],
        out_specs=pl.BlockSpec((1, ATT_QBLK, hps * DV), lambda b, g, i: (b, i, g)),
        out_shape=jax.ShapeDtypeStruct((B, S, HEADS_W), BF16),
        scratch_shapes=[pltpu.VMEM((chains, ATT_BLK, ATT_QBLK), BF16),
                        pltpu.VMEM((chains, ATT_BLK, ATT_QBLK), BF16),
                        pltpu.VMEM((chains, 1, ATT_QBLK), F32),
                        pltpu.VMEM((chains, 1, ATT_QBLK), F32),
                        pltpu.VMEM((chains, 1, ATT_QBLK), F32),
                        pltpu.VMEM((chains, DV_AUG, ATT_QBLK), F32)],
        compiler_params=pltpu.CompilerParams(
            dimension_semantics=("arbitrary", "arbitrary", "arbitrary"),
            vmem_limit_bytes=VMEM_LIMIT_ATT),
        name="diff_attn",
    )(qt, kb, vt, lq1, lk1, lq2, lk2, sw)


def _out_ffn_kernel(x_ref, oa_ref, ob_ref, wout_ref, n2w_ref, wup_ref, cw_ref, cb_ref, wdown_ref,
                    out_ref, ext_ref, act_ref):
    tm = x_ref.shape[1]
    ffn = wdown_ref.shape[0]
    fb = FFN_BLK
    s_idx = pl.program_id(1)

    @pl.when(s_idx == 0)
    def _():
        ext_ref[0:CONV_PAD, :] = jnp.zeros((CONV_PAD, ext_ref.shape[1]), F32)

    @pl.when(s_idx > 0)
    def _():
        ext_ref[0:CONV_PAD, :] = ext_ref[tm:tm + CONV_PAD, :]

    mix = jnp.concatenate([oa_ref[0], ob_ref[0]], axis=-1)
    h1 = x_ref[0] + _dot(mix, wout_ref[...])
    ms = jnp.mean(h1 * h1, axis=-1, keepdims=True)
    hn = (h1 * lax.rsqrt(ms + EPS) * n2w_ref[...]).astype(BF16)

    ext_ref[CONV_PAD:CONV_PAD + tm, :] = _dot(hn, wup_ref[...])

    def conv_cols(lo):
        cs = slice(lo, lo + fb)
        y = cb_ref[:, cs]
        for j in range(FFN_CONV):
            off = CONV_PAD - (FFN_CONV - 1) + j
            y = y + cw_ref[j:j + 1, cs] * ext_ref[off:off + tm, cs]
        return y

    for ci in range(ffn // fb):
        gate = conv_cols(ci * fb)
        up = conv_cols(ffn + ci * fb)
        act_ref[:, ci * fb:(ci + 1) * fb] = (_silu(gate) * up).astype(BF16)

    out_ref[0] = h1 + _dot(act_ref[...], wdown_ref[...])


def _out_ffn(x, oa, ob, wout, n2w, wup, cw, cb, wdown):
    B, S, D = x.shape
    tm = FFN_TOK_TILE
    ffn = wdown.shape[0]
    tok = lambda w: pl.BlockSpec((1, tm, w), lambda b, s: (b, s, 0))
    full = lambda a: pl.BlockSpec(a.shape, lambda b, s: (0,) * a.ndim,
                                  pipeline_mode=pl.Buffered(1))
    return pl.pallas_call(
        _out_ffn_kernel,
        grid=(B, S // tm),
        in_specs=[tok(D), tok(HEADS_W), tok(HEADS_W), full(wout), full(n2w), full(wup), full(cw),
                  full(cb), full(wdown)],
        out_specs=tok(D),
        out_shape=jax.ShapeDtypeStruct((B, S, D), F32),
        scratch_shapes=[pltpu.VMEM((tm + CONV_PAD, 2 * ffn), F32),
                        pltpu.VMEM((tm, ffn), BF16)],
        compiler_params=pltpu.CompilerParams(
            dimension_semantics=("arbitrary", "arbitrary"), vmem_limit_bytes=VMEM_LIMIT_FFN),
        name="out_ffn",
    )(x, oa, ob, wout, n2w, wup, cw, cb, wdown)


def _lane_place(v, offset):
    return jnp.zeros((1, LANES), F32).at[0, offset:offset + v.shape[0]].set(v.astype(F32))


def kernel(x, positions, norm1_w, w_in, dn_conv_w, dn_a_log, dn_dt_bias, dn_norm_w, df_q_norm_w,
           df_k_norm_w, df_lambda_q1, df_lambda_k1, df_lambda_q2, df_lambda_k2, df_subln_w, w_out,
           norm2_w, w_up, ffn_conv_w, ffn_conv_b, w_down):
    B, S, D = x.shape
    depth = w_in.shape[0]
    w = HEADS_W
    h = x
    pos3 = positions.reshape(B, 1, S)
    inv_freq = ROPE_THETA ** (-jnp.arange(0, ROPE_DIM, 2, dtype=F32) / ROPE_DIM)
    invf = jnp.broadcast_to(inv_freq[:, None], (ROPE_HALF, ATT_BLK))
    for l in range(depth):
        lam_init = 0.8 - 0.6 * math.exp(-0.3 * l)
        wi = w_in[l].astype(BF16)
        wqkv = wi[:, 0:3 * w]
        wz = wi[:, 3 * w:4 * w]
        n_gate = 2 * DN_HEADS
        wab = jnp.pad(wi[:, 4 * w:4 * w + n_gate], ((0, 0), (0, LANES - n_gate)))
        base = 4 * w + n_gate
        wfqt = wi[:, base:base + w].T
        wfkt = wi[:, base + w:base + 2 * w].T
        wfvt = wi[:, base + 2 * w:base + 3 * w].T
        qnw = jnp.broadcast_to(df_q_norm_w[l].astype(F32)[:, None], (DF_HEAD, ATT_BLK))
        knw = jnp.broadcast_to(df_k_norm_w[l].astype(F32)[:, None], (DF_HEAD, ATT_BLK))

        qa, ka, va, zs, ab, qt, kb, vt = _in_proj(
            h, pos3, norm1_w[l][None, :], wqkv, wz, wab, wfqt, wfkt, wfvt, dn_conv_w[l], qnw, knw,
            invf)

        o_a = _deltanet(qa, ka, va, zs, ab, _lane_place(dn_a_log[l], 0),
                        _lane_place(dn_dt_bias[l], 0), dn_norm_w[l][None, :].astype(F32))

        sw = jnp.broadcast_to(df_subln_w[l].astype(F32)[:, None], (2 * DF_HEAD, ATT_QBLK))
        o_b = _diff_attn(qt, kb, vt, df_lambda_q1[l][None, :], df_lambda_k1[l][None, :],
                         df_lambda_q2[l][None, :], df_lambda_k2[l][None, :], sw, lam_init)

        h = _out_ffn(h, o_a, o_b, w_out[l].astype(BF16), norm2_w[l][None, :],
                     w_up[l].astype(BF16), ffn_conv_w[l], ffn_conv_b[l][None, :],
                     w_down[l].astype(BF16))
    return h
```

```python
import functools
import math

import jax
import jax.numpy as jnp
from jax import lax
from jax.experimental import pallas as pl
from jax.experimental.pallas import tpu as pltpu

F32 = jnp.float32
BF16 = jnp.bfloat16

EPS = 1e-6
ROPE_THETA = 500000.0

DN_HEADS = 4
DN_HEAD = 128
DN_CONV = 4
DF_HEADS = 4
DF_HEAD = 64
ROPE_DIM = DF_HEAD // 4
ROPE_HALF = ROPE_DIM // 2
FFN_CONV = 3
HEADS_W = 512

LANES = 128
SUBLANES = 8

TOK_TILE = 1024
FFN_TOK_TILE = 512
DN_TILE = 256
DN_CHUNK = 128
DN_SEQS = 4
INV_BASE = 16
ATT_BLK = 256
ATT_QBLK = 2 * ATT_BLK
FFN_BLK = 256
CONV_PAD = SUBLANES
ATT_HEADS_PER_STEP = 4
DV = 2 * DF_HEAD
BF16_SUBLANES = 2 * SUBLANES
DV_AUG = DV + BF16_SUBLANES
LOG2E = math.log2(math.e)

VMEM_LIMIT_PROJ = 48 * 1024 * 1024
VMEM_LIMIT_DN = 32 * 1024 * 1024
VMEM_LIMIT_ATT = 40 * 1024 * 1024
VMEM_LIMIT_FFN = 56 * 1024 * 1024


def _dot(a, b):
    return jnp.dot(a, b, preferred_element_type=F32)


def _dot_nt(a, b):
    return lax.dot_general(a, b, (((1,), (1,)), ((), ())), preferred_element_type=F32)


def _dot_tn(a, b):
    return lax.dot_general(a, b, (((0,), (0,)), ((), ())), preferred_element_type=F32)


def _silu(x):
    h = 0.5 * x
    return h + h * jnp.tanh(h)


def _softplus(x):
    return jnp.maximum(x, 0.0) + jnp.log(1.0 + jnp.exp(-jnp.abs(x)))


def _qk_norm_rope(t, nw, cos, sin):
    n = t.shape[-1]
    t3 = t.reshape(2 * DF_HEADS, DF_HEAD, n)
    ms = jnp.mean(t3 * t3, axis=1, keepdims=True)
    t3 = t3 * lax.rsqrt(ms + EPS) * nw[None]
    x1 = t3[:, 0:ROPE_HALF, :]
    x2 = t3[:, ROPE_HALF:ROPE_DIM, :]
    c = cos[None]
    s = sin[None]
    out = jnp.concatenate([x1 * c - x2 * s, x2 * c + x1 * s, t3[:, ROPE_DIM:, :]], axis=1)
    return out.reshape(2 * DF_HEADS * DF_HEAD, n)


def _in_proj_kernel(x_ref, pos_ref, n1w_ref, wqkv_ref, wz_ref, wab_ref, wfqt_ref, wfkt_ref,
                    wfvt_ref, convw_ref, qnw_ref, knw_ref, invf_ref,
                    qa_ref, ka_ref, va_ref, z_ref, ab_ref, qt_ref, kb_ref, vt_ref,
                    ext_ref):
    tm = x_ref.shape[1]
    s_idx = pl.program_id(1)

    @pl.when(s_idx == 0)
    def _():
        ext_ref[0:CONV_PAD, :] = jnp.zeros((CONV_PAD, ext_ref.shape[1]), F32)

    @pl.when(s_idx > 0)
    def _():
        ext_ref[0:CONV_PAD, :] = ext_ref[tm:tm + CONV_PAD, :]

    x = x_ref[0]
    ms = jnp.mean(x * x, axis=-1, keepdims=True)
    xn = (x * lax.rsqrt(ms + EPS) * n1w_ref[...]).astype(BF16)

    z_ref[0] = _silu(_dot(xn, wz_ref[...])).astype(BF16)
    ab_ref[0] = _dot(xn, wab_ref[...])
    outs = (qa_ref, ka_ref, va_ref)
    t_weights = (wfqt_ref, wfkt_ref, wfvt_ref)
    t_proj = []
    for seg in range(3):
        lo = seg * HEADS_W
        u = _dot(xn, wqkv_ref[:, lo:lo + HEADS_W])
        ext_ref[CONV_PAD:CONV_PAD + tm, lo:lo + HEADS_W] = u
        t_proj.append(_dot_nt(t_weights[seg][...], xn))
        y = convw_ref[DN_CONV - 1:DN_CONV, lo:lo + HEADS_W] * u
        for j in range(DN_CONV - 1):
            off = CONV_PAD - (DN_CONV - 1) + j
            y = y + convw_ref[j:j + 1, lo:lo + HEADS_W] * ext_ref[off:off + tm, lo:lo + HEADS_W]
        y = _silu(y)
        for h in range(DN_HEADS):
            blk = y[:, h * DN_HEAD:(h + 1) * DN_HEAD]
            if seg < 2:
                ss = jnp.sum(blk * blk, axis=-1, keepdims=True)
                scale = lax.rsqrt(ss + EPS)
                if seg == 0:
                    scale = scale * (DN_HEAD ** -0.5)
                blk = blk * scale
            outs[seg][0, :, h * DN_HEAD:(h + 1) * DN_HEAD] = blk.astype(BF16)

    n_blk = tm // ATT_BLK
    pos = pos_ref[0].astype(F32)
    qt, kt, vt = t_proj
    for i in range(n_blk):
        sl = slice(i * ATT_BLK, (i + 1) * ATT_BLK)
        ang = pos[:, sl] * invf_ref[...]
        cos = jnp.cos(ang)
        sin = jnp.sin(ang)
        q_i = _qk_norm_rope(qt[:, sl], qnw_ref[...] * (DF_HEAD ** -0.5 * LOG2E), cos, sin)
        k_i = _qk_norm_rope(kt[:, sl], knw_ref[...], cos, sin)
        qt_ref[0, i] = q_i.astype(BF16)
        kb_ref[0, sl, :] = k_i.T.astype(BF16)
        ones_rows = (lax.broadcasted_iota(jnp.int32, (BF16_SUBLANES, ATT_BLK), 0) == 0).astype(BF16)
        for h in range(DF_HEADS):
            vt_ref[0, i, h * DV_AUG:h * DV_AUG + DV, :] = vt[h * DV:(h + 1) * DV, sl].astype(BF16)
            vt_ref[0, i, h * DV_AUG + DV:(h + 1) * DV_AUG, :] = ones_rows


def _in_proj(x, pos3, n1w, wqkv, wz, wab, wfqt, wfkt, wfvt, convw, qnw, knw, invf):
    B, S, D = x.shape
    tm = TOK_TILE
    n_blk = tm // ATT_BLK
    grid = (B, S // tm)
    full = lambda a: pl.BlockSpec(a.shape, lambda b, s: (0,) * a.ndim)
    tok = lambda w: pl.BlockSpec((1, tm, w), lambda b, s: (b, s, 0))
    tpose = pl.BlockSpec((1, n_blk, HEADS_W, ATT_BLK), lambda b, s: (b, s, 0, 0))
    slab = jax.ShapeDtypeStruct((B, S, HEADS_W), BF16)
    slab_t = jax.ShapeDtypeStruct((B, S // ATT_BLK, HEADS_W, ATT_BLK), BF16)
    v_rows = DF_HEADS * DV_AUG
    tpose_v = pl.BlockSpec((1, n_blk, v_rows, ATT_BLK), lambda b, s: (b, s, 0, 0))
    slab_tv = jax.ShapeDtypeStruct((B, S // ATT_BLK, v_rows, ATT_BLK), BF16)
    return pl.pallas_call(
        _in_proj_kernel,
        grid=grid,
        in_specs=[tok(D), pl.BlockSpec((1, 1, tm), lambda b, s: (b, 0, s)), full(n1w), full(wqkv),
                  full(wz), full(wab), full(wfqt), full(wfkt), full(wfvt), full(convw), full(qnw),
                  full(knw), full(invf)],
        out_specs=[tok(HEADS_W), tok(HEADS_W), tok(HEADS_W), tok(HEADS_W), tok(LANES),
                   tpose, tok(HEADS_W), tpose_v],
        out_shape=[slab, slab, slab, slab, jax.ShapeDtypeStruct((B, S, LANES), F32),
                   slab_t, slab, slab_tv],
        scratch_shapes=[pltpu.VMEM((tm + CONV_PAD, 3 * HEADS_W), F32)],
        compiler_params=pltpu.CompilerParams(
            dimension_semantics=("arbitrary", "arbitrary"), vmem_limit_bytes=VMEM_LIMIT_PROJ),
        name="in_proj",
    )(x, pos3, n1w, wqkv, wz, wab, wfqt, wfkt, wfvt, convw, qnw, knw, invf)


def _unit_lower_inverse(mats, row, col):
    c = mats[0].shape[0]
    eye = (row == col).astype(F32)

    def same_block(size):
        return (row // size) == (col // size)

    base = same_block(INV_BASE)
    ps = [jnp.where(base, a, 0.0) for a in mats]
    ns = [eye - d for d in ps]
    width = 2
    while width < INV_BASE:
        pbs = [p.astype(BF16) for p in ps]
        ps = [_dot(pb, pb) for pb in pbs]
        ns = [n + _dot(n.astype(BF16), p.astype(BF16)) for n, p in zip(ns, ps)]
        width *= 2
    size = INV_BASE
    while size < c:
        join = same_block(2 * size) & jnp.logical_not(same_block(size))
        nbs = [n.astype(BF16) for n in ns]
        nes = [_dot(nb, jnp.where(join, a, 0.0).astype(BF16)) for nb, a in zip(nbs, mats)]
        ns = [n - _dot(ne.astype(BF16), nb) for n, ne, nb in zip(ns, nes, nbs)]
        size *= 2
    return [n - eye for n in ns]


def _deltanet_kernel(q_ref, k_ref, v_ref, z_ref, ab_ref, alog_ref, dtb_ref, nw_ref, o_ref, state_ref):
    tc = q_ref.shape[1]
    c = DN_CHUNK

    @pl.when(pl.program_id(1) == 0)
    def _():
        state_ref[...] = jnp.zeros(state_ref.shape, F32)

    n_seq = q_ref.shape[0]
    r_t = lax.broadcasted_iota(jnp.int32, (tc, tc), 0)
    c_t = lax.broadcasted_iota(jnp.int32, (tc, tc), 1)
    tril_bd = ((r_t >= c_t) & ((r_t // c) == (c_t // c))).astype(BF16)
    gcum_b, beta_b, gcum_t = [], [], []
    for bi in range(n_seq):
        ab = ab_ref[bi]
        g = -jnp.exp(alog_ref[...]) * _softplus(ab + dtb_ref[...])
        beta = jax.nn.sigmoid(ab)
        gcum = _split_dot_left(tril_bd, g)
        gcum_b.append(jnp.concatenate(
            [jnp.broadcast_to(gcum[:, h:h + 1], (tc, DN_HEAD)) for h in range(DN_HEADS)], axis=1))
        beta_b.append(jnp.concatenate(
            [jnp.broadcast_to(beta[:, DN_HEADS + h:DN_HEADS + h + 1], (tc, DN_HEAD))
             for h in range(DN_HEADS)], axis=1))
        gcum_t.append(gcum.T)

    row = lax.broadcasted_iota(jnp.int32, (c, c), 0)
    col = lax.broadcasted_iota(jnp.int32, (c, c), 1)
    lower = row >= col
    strict = row > col

    chains = [(bi, h) for bi in range(n_seq) for h in range(DN_HEADS)]
    tiles = [(bi, slice(ci * c, (ci + 1) * c), h, slice(h * DN_HEAD, (h + 1) * DN_HEAD))
             for ci in range(tc // c) for bi, h in chains]
    qs = [q_ref[bi, rs, hs] for bi, rs, _, hs in tiles]
    ks = [k_ref[bi, rs, hs] for bi, rs, _, hs in tiles]
    g_cols = [gcum_b[bi][rs, hs] for bi, rs, _, hs in tiles]
    b_cols = [beta_b[bi][rs, hs] for bi, rs, _, hs in tiles]
    decays = [jnp.where(lower, jnp.exp(jnp.minimum(g_col - gcum_t[bi][h:h + 1, rs], 0.0)), 0.0)
              for g_col, (bi, rs, h, _) in zip(g_cols, tiles)]
    k_betas = [k.astype(F32) * b_col for k, b_col in zip(ks, b_cols)]
    a_mats = [jnp.where(strict, _dot_nt(kb.astype(BF16), k) * dec, 0.0)
              for kb, k, dec in zip(k_betas, ks, decays)]
    intras = [(_dot_nt(q, k) * dec).astype(BF16) for q, k, dec in zip(qs, ks, decays)]
    t_offs = _unit_lower_inverse(a_mats, row, col)
    e_gs = [jnp.exp(g_col) for g_col in g_cols]
    rhss = [jnp.concatenate([v_ref[bi, rs, hs].astype(F32) * b_col, kb * e_g], axis=1)
            for (bi, rs, _, hs), b_col, kb, e_g in zip(tiles, b_cols, k_betas, e_gs)]
    sols = [rhs + _dot(t.astype(BF16), rhs.astype(BF16)) for t, rhs in zip(t_offs, rhss)]
    q_decs = [(q.astype(F32) * e_g).astype(BF16) for q, e_g in zip(qs, e_gs)]
    g_lasts = [g_col[c - 1:c, :] for g_col in g_cols]
    k_decs = [(k.astype(F32) * jnp.exp(g_last - g_col)).astype(BF16)
              for k, g_last, g_col in zip(ks, g_lasts, g_cols)]

    n_ch = len(chains)
    for ci in range(tc // c):
        idx = [ci * n_ch + j for j in range(n_ch)]
        states = [state_ref[bi, h] for bi, h in chains]
        state_bs = [st.astype(BF16) for st in states]
        v_news = [sols[i][:, :DN_HEAD] - _dot(sols[i][:, DN_HEAD:].astype(BF16), sb)
                  for i, sb in zip(idx, state_bs)]
        inter = [_dot(q_decs[i], sb) for i, sb in zip(idx, state_bs)]
        v_new_bs = [vn.astype(BF16) for vn in v_news]
        outs = [o + _dot(intras[i], vb) for i, o, vb in zip(idx, inter, v_new_bs)]
        for j, i in enumerate(idx):
            bi, h = chains[j]
            state_ref[bi, h] = states[j] * jnp.exp(g_lasts[i]) + _dot_tn(k_decs[i], v_new_bs[j])
        for j, i in enumerate(idx):
            bi, rs, _, hs = tiles[i]
            o = outs[j]
            ms = jnp.mean(o * o, axis=-1, keepdims=True)
            o = o * lax.rsqrt(ms + EPS) * nw_ref[...]
            o_ref[bi, rs, hs] = (o * z_ref[bi, rs, hs].astype(F32)).astype(BF16)


def _split_dot_left(m_bf16, a):
    acc = None
    rem = a
    for _ in range(3):
        part = rem.astype(BF16)
        d = _dot(m_bf16, part)
        acc = d if acc is None else acc + d
        rem = rem - part.astype(F32)
    return acc


def _deltanet(qa, ka, va, zs, ab, alog_n, dtb_n, nw):
    B, S, _ = qa.shape
    tc = DN_TILE
    n_seq = DN_SEQS if B % DN_SEQS == 0 else 1
    tok = lambda w: pl.BlockSpec((n_seq, tc, w), lambda b, s: (b, s, 0))
    full = lambda a: pl.BlockSpec(a.shape, lambda b, s: (0,) * a.ndim)
    return pl.pallas_call(
        _deltanet_kernel,
        grid=(B // n_seq, S // tc),
        in_specs=[tok(HEADS_W), tok(HEADS_W), tok(HEADS_W), tok(HEADS_W), tok(LANES),
                  full(alog_n), full(dtb_n), full(nw)],
        out_specs=tok(HEADS_W),
        out_shape=jax.ShapeDtypeStruct((B, S, HEADS_W), BF16),
        scratch_shapes=[pltpu.VMEM((n_seq, DN_HEADS, DN_HEAD, DN_HEAD), F32)],
        compiler_params=pltpu.CompilerParams(
            dimension_semantics=("arbitrary", "arbitrary"), vmem_limit_bytes=VMEM_LIMIT_DN),
        name="deltanet",
    )(qa, ka, va, zs, ab, alog_n, dtb_n, nw)


def _diff_attn_kernel(qt_ref, k_ref, vt_ref, lq1_ref, lk1_ref, lq2_ref, lk2_ref, sw_ref, o_ref,
                      pa_ref, pb_ref, ala_ref, alb_ref, m_ref, acc_ref, *, lam_init):
    blk = ATT_BLK
    n_sub = ATT_QBLK // ATT_BLK
    iq = pl.program_id(2)
    hd = 2 * DF_HEAD

    q_maps = []
    for hh in range(ATT_HEADS_PER_STEP):
        qt = jnp.concatenate([qt_ref[0, i, hh * hd:(hh + 1) * hd, :] for i in range(n_sub)], axis=1)
        d_idx = lax.broadcasted_iota(jnp.int32, qt.shape, 0)
        zero = jnp.zeros_like(qt)
        q_maps.append(jnp.where(d_idx < DF_HEAD, qt, zero))
        q_maps.append(jnp.where(d_idx >= DF_HEAD, qt, zero))

    def probs(j, p_ref, al_ref, mask_lane0=None):
        lane0 = mask_lane0 or 0
        rows = pl.ds(pl.multiple_of(j * blk, blk), blk)
        for hh in range(ATT_HEADS_PER_STEP):
            k_blk = k_ref[0, rows, hh * hd:(hh + 1) * hd]
            for m_i in range(2):
                c = 2 * hh + m_i
                s = _dot(k_blk, q_maps[c][:, lane0:])
                if mask_lane0 is not None:
                    kpos = lax.broadcasted_iota(jnp.int32, s.shape, 0)
                    qpos = lax.broadcasted_iota(jnp.int32, s.shape, 1)
                    s = jnp.where(kpos <= qpos, s, -jnp.inf)
                s = s.astype(BF16)
                m_old = m_ref[c, :, lane0:]
                m_new = jnp.maximum(m_old, jnp.max(s, axis=0, keepdims=True).astype(F32))
                m_ref[c, :, lane0:] = m_new
                al_ref[c, :, lane0:] = jnp.exp2(m_old - m_new)
                p_ref[c, :, lane0:] = jnp.exp2(s - m_new.astype(BF16))

    def values(j, p_ref, al_ref, lane0=0):
        for hh in range(ATT_HEADS_PER_STEP):
            v_blk = vt_ref[0, j, hh * DV_AUG:(hh + 1) * DV_AUG, :]
            for m_i in range(2):
                c = 2 * hh + m_i
                acc_ref[c, :, lane0:] = (al_ref[c, :, lane0:] * acc_ref[c, :, lane0:]
                                         + _dot(v_blk, p_ref[c, :, lane0:]))

    m_ref[...] = jnp.full(m_ref.shape, -jnp.inf, F32)
    acc_ref[...] = jnp.zeros(acc_ref.shape, F32)
    diag0 = n_sub * iq
    bufs = ((pa_ref, ala_ref), (pb_ref, alb_ref))

    def drain(first_pending, blocks):
        pending = first_pending
        for j, par, lane0 in blocks:
            probs(j, *bufs[par], mask_lane0=lane0)
            if pending is not None:
                values(pending[0], *bufs[pending[1]], lane0=pending[2] or 0)
            pending = (j, par, lane0)
        values(pending[0], *bufs[pending[1]], lane0=pending[2] or 0)

    diag_blocks = [(diag0 + d, d % 2, d * blk) for d in range(n_sub)]

    @pl.when(iq == 0)
    def _():
        drain(None, diag_blocks)

    @pl.when(iq > 0)
    def _():
        probs(0, pa_ref, ala_ref)

        def pair(t, _):
            a = 2 * t
            probs(a + 1, pb_ref, alb_ref)
            values(a, pa_ref, ala_ref)
            probs(a + 2, pa_ref, ala_ref)
            values(a + 1, pb_ref, alb_ref)
            return 0

        lax.fori_loop(0, (n_sub // 2) * iq - 1, pair, 0)

        drain((diag0 - 2, 0, None), [(diag0 - 1, 1, None)] + diag_blocks)

    lam = (jnp.exp(jnp.sum(lq1_ref[...] * lk1_ref[...], axis=-1, keepdims=True))
           - jnp.exp(jnp.sum(lq2_ref[...] * lk2_ref[...], axis=-1, keepdims=True)) + lam_init)
    for hh in range(ATT_HEADS_PER_STEP):
        a1 = acc_ref[2 * hh]
        a2 = acc_ref[2 * hh + 1]
        inv1 = 1.0 / a1[DV:DV + 1]
        inv2 = lam / a2[DV:DV + 1]
        o = a1[:DV] * inv1 - a2[:DV] * inv2
        ms = jnp.mean(o * o, axis=0, keepdims=True)
        o = o * lax.rsqrt(ms + EPS) * sw_ref[...] * (1.0 - lam_init)
        o_ref[0, :, hh * DV:(hh + 1) * DV] = o.T.astype(BF16)


def _diff_attn(qt, kb, vt, lq1, lk1, lq2, lk2, sw, lam_init):
    B, S, _ = kb.shape
    nb = S // ATT_BLK
    hps = ATT_HEADS_PER_STEP
    chains = 2 * hps
    full = lambda a: pl.BlockSpec(a.shape, lambda b, g, i: (0,) * a.ndim)
    return pl.pallas_call(
        functools.partial(_diff_attn_kernel, lam_init=lam_init),
        grid=(B, DF_HEADS // hps, S // ATT_QBLK),
        in_specs=[pl.BlockSpec((1, ATT_QBLK // ATT_BLK, hps * DV, ATT_BLK),
                               lambda b, g, i: (b, i, g, 0)),
                  pl.BlockSpec((1, S, hps * DV), lambda b, g, i: (b, 0, g)),
                  pl.BlockSpec((1, nb, hps * DV_AUG, ATT_BLK), lambda b, g, i: (b, 0, g, 0)),
                  full(lq1), full(lk1), full(lq2), full(lk2), full(sw)],
        out_specs=pl.BlockSpec((1, ATT_QBLK, hps * DV), lambda b, g, i: (b, i, g)),
        out_shape=jax.ShapeDtypeStruct((B, S, HEADS_W), BF16),
        scratch_shapes=[pltpu.VMEM((chains, ATT_BLK, ATT_QBLK), BF16),
                        pltpu.VMEM((chains, ATT_BLK, ATT_QBLK), BF16),
                        pltpu.VMEM((chains, 1, ATT_QBLK), F32),
                        pltpu.VMEM((chains, 1, ATT_QBLK), F32),
                        pltpu.VMEM((chains, 1, ATT_QBLK), F32),
                        pltpu.VMEM((chains, DV_AUG, ATT_QBLK), F32)],
        compiler_params=pltpu.CompilerParams(
            dimension_semantics=("arbitrary", "arbitrary", "arbitrary"),
            vmem_limit_bytes=VMEM_LIMIT_ATT),
        name="diff_attn",
    )(qt, kb, vt, lq1, lk1, lq2, lk2, sw)


def _out_ffn_kernel(x_ref, oa_ref, ob_ref, wout_ref, n2w_ref, wup_ref, cw_ref, cb_ref, wdown_ref,
                    out_ref, ext_ref, act_ref):
    tm = x_ref.shape[1]
    ffn = wdown_ref.shape[0]
    fb = FFN_BLK
    s_idx = pl.program_id(1)

    @pl.when(s_idx == 0)
    def _():
        ext_ref[0:CONV_PAD, :] = jnp.zeros((CONV_PAD, ext_ref.shape[1]), F32)

    @pl.when(s_idx > 0)
    def _():
        ext_ref[0:CONV_PAD, :] = ext_ref[tm:tm + CONV_PAD, :]

    mix = jnp.concatenate([oa_ref[0], ob_ref[0]], axis=-1)
    h1 = x_ref[0] + _dot(mix, wout_ref[...])
    ms = jnp.mean(h1 * h1, axis=-1, keepdims=True)
    hn = (h1 * lax.rsqrt(ms + EPS) * n2w_ref[...]).astype(BF16)

    ext_ref[CONV_PAD:CONV_PAD + tm, :] = _dot(hn, wup_ref[...])

    def conv_cols(lo):
        cs = slice(lo, lo + fb)
        y = cb_ref[:, cs]
        for j in range(FFN_CONV):
            off = CONV_PAD - (FFN_CONV - 1) + j
            y = y + cw_ref[j:j + 1, cs] * ext_ref[off:off + tm, cs]
        return y

    for ci in range(ffn // fb):
        gate = conv_cols(ci * fb)
        up = conv_cols(ffn + ci * fb)
        act_ref[:, ci * fb:(ci + 1) * fb] = (_silu(gate) * up).astype(BF16)

    out_ref[0] = h1 + _dot(act_ref[...], wdown_ref[...])


def _out_ffn(x, oa, ob, wout, n2w, wup, cw, cb, wdown):
    B, S, D = x.shape
    tm = FFN_TOK_TILE
    ffn = wdown.shape[0]
    tok = lambda w: pl.BlockSpec((1, tm, w), lambda b, s: (b, s, 0))
    full = lambda a: pl.BlockSpec(a.shape, lambda b, s: (0,) * a.ndim,
                                  pipeline_mode=pl.Buffered(1))
    return pl.pallas_call(
        _out_ffn_kernel,
        grid=(B, S // tm),
        in_specs=[tok(D), tok(HEADS_W), tok(HEADS_W), full(wout), full(n2w), full(wup), full(cw),
                  full(cb), full(wdown)],
        out_specs=tok(D),
        out_shape=jax.ShapeDtypeStruct((B, S, D), F32),
        scratch_shapes=[pltpu.VMEM((tm + CONV_PAD, 2 * ffn), F32),
                        pltpu.VMEM((tm, ffn), BF16)],
        compiler_params=pltpu.CompilerParams(
            dimension_semantics=("arbitrary", "arbitrary"), vmem_limit_bytes=VMEM_LIMIT_FFN),
        name="out_ffn",
    )(x, oa, ob, wout, n2w, wup, cw, cb, wdown)


def _lane_place(v, offset):
    return jnp.zeros((1, LANES), F32).at[0, offset:offset + v.shape[0]].set(v.astype(F32))


def kernel(x, positions, norm1_w, w_in, dn_conv_w, dn_a_log, dn_dt_bias, dn_norm_w, df_q_norm_w,
           df_k_norm_w, df_lambda_q1, df_lambda_k1, df_lambda_q2, df_lambda_k2, df_subln_w, w_out,
           norm2_w, w_up, ffn_conv_w, ffn_conv_b, w_down):
    B, S, D = x.shape
    depth = w_in.shape[0]
    w = HEADS_W
    h = x
    pos3 = positions.reshape(B, 1, S)
    inv_freq = ROPE_THETA ** (-jnp.arange(0, ROPE_DIM, 2, dtype=F32) / ROPE_DIM)
    invf = jnp.broadcast_to(inv_freq[:, None], (ROPE_HALF, ATT_BLK))
    for l in range(depth):
        lam_init = 0.8 - 0.6 * math.exp(-0.3 * l)
        wi = w_in[l].astype(BF16)
        wqkv = wi[:, 0:3 * w]
        wz = wi[:, 3 * w:4 * w]
        n_gate = 2 * DN_HEADS
        wab = jnp.pad(wi[:, 4 * w:4 * w + n_gate], ((0, 0), (0, LANES - n_gate)))
        base = 4 * w + n_gate
        wfqt = wi[:, base:base + w].T
        wfkt = wi[:, base + w:base + 2 * w].T
        wfvt = wi[:, base + 2 * w:base + 3 * w].T
        qnw = jnp.broadcast_to(df_q_norm_w[l].astype(F32)[:, None], (DF_HEAD, ATT_BLK))
        knw = jnp.broadcast_to(df_k_norm_w[l].astype(F32)[:, None], (DF_HEAD, ATT_BLK))

        qa, ka, va, zs, ab, qt, kb, vt = _in_proj(
            h, pos3, norm1_w[l][None, :], wqkv, wz, wab, wfqt, wfkt, wfvt, dn_conv_w[l], qnw, knw,
            invf)

        o_a = _deltanet(qa, ka, va, zs, ab, _lane_place(dn_a_log[l], 0),
                        _lane_place(dn_dt_bias[l], 0), dn_norm_w[l][None, :].astype(F32))

        sw = jnp.broadcast_to(df_subln_w[l].astype(F32)[:, None], (2 * DF_HEAD, ATT_QBLK))
        o_b = _diff_attn(qt, kb, vt, df_lambda_q1[l][None, :], df_lambda_k1[l][None, :],
                         df_lambda_q2[l][None, :], df_lambda_k2[l][None, :], sw, lam_init)

        h = _out_ffn(h, o_a, o_b, w_out[l].astype(BF16), norm2_w[l][None, :],
                     w_up[l].astype(BF16), ffn_conv_w[l], ffn_conv_b[l][None, :],
                     w_down[l].astype(BF16))
    return h
```

```python
import functools
import math

import jax
import jax.numpy as jnp
from jax import lax
from jax.experimental import pallas as pl
from jax.experimental.pallas import tpu as pltpu

F32 = jnp.float32
BF16 = jnp.bfloat16

EPS = 1e-6
ROPE_THETA = 500000.0

DN_HEADS = 4
DN_HEAD = 128
DN_CONV = 4
DF_HEADS = 4
DF_HEAD = 64
ROPE_DIM = DF_HEAD // 4
ROPE_HALF = ROPE_DIM // 2
FFN_CONV = 3
HEADS_W = 512

LANES = 128
SUBLANES = 8

TOK_TILE = 1024
FFN_TOK_TILE = 512
DN_TILE = 256
DN_CHUNK = 128
DN_SEQS = 4
INV_BASE = 16
ATT_BLK = 256
ATT_QBLK = 2 * ATT_BLK
FFN_BLK = 256
CONV_PAD = SUBLANES
ATT_HEADS_PER_STEP = 4
DV = 2 * DF_HEAD
BF16_SUBLANES = 2 * SUBLANES
DV_AUG = DV + BF16_SUBLANES
LOG2E = math.log2(math.e)

VMEM_LIMIT_PROJ = 48 * 1024 * 1024
VMEM_LIMIT_DN = 32 * 1024 * 1024
VMEM_LIMIT_ATT = 40 * 1024 * 1024
VMEM_LIMIT_FFN = 56 * 1024 * 1024


def _dot(a, b):
    return jnp.dot(a, b, preferred_element_type=F32)


def _dot_nt(a, b):
    return lax.dot_general(a, b, (((1,), (1,)), ((), ())), preferred_element_type=F32)


def _dot_tn(a, b):
    return lax.dot_general(a, b, (((0,), (0,)), ((), ())), preferred_element_type=F32)


def _silu(x):
    h = 0.5 * x
    return h + h * jnp.tanh(h)


def _softplus(x):
    return jnp.maximum(x, 0.0) + jnp.log(1.0 + jnp.exp(-jnp.abs(x)))


def _qk_norm_rope(t, nw, cos, sin):
    n = t.shape[-1]
    t3 = t.reshape(2 * DF_HEADS, DF_HEAD, n)
    ms = jnp.mean(t3 * t3, axis=1, keepdims=True)
    t3 = t3 * lax.rsqrt(ms + EPS) * nw[None]
    x1 = t3[:, 0:ROPE_HALF, :]
    x2 = t3[:, ROPE_HALF:ROPE_DIM, :]
    c = cos[None]
    s = sin[None]
    out = jnp.concatenate([x1 * c - x2 * s, x2 * c + x1 * s, t3[:, ROPE_DIM:, :]], axis=1)
    return out.reshape(2 * DF_HEADS * DF_HEAD, n)


def _in_proj_kernel(x_ref, pos_ref, n1w_ref, wqkv_ref, wz_ref, wab_ref, wfqt_ref, wfkt_ref,
                    wfvt_ref, convw_ref, qnw_ref, knw_ref, invf_ref,
                    qa_ref, ka_ref, va_ref, z_ref, ab_ref, qt_ref, kb_ref, vt_ref,
                    ext_ref):
    tm = x_ref.shape[1]
    s_idx = pl.program_id(1)

    @pl.when(s_idx == 0)
    def _():
        ext_ref[0:CONV_PAD, :] = jnp.zeros((CONV_PAD, ext_ref.shape[1]), F32)

    @pl.when(s_idx > 0)
    def _():
        ext_ref[0:CONV_PAD, :] = ext_ref[tm:tm + CONV_PAD, :]

    x = x_ref[0]
    ms = jnp.mean(x * x, axis=-1, keepdims=True)
    xn = (x * lax.rsqrt(ms + EPS) * n1w_ref[...]).astype(BF16)

    z_ref[0] = _silu(_dot(xn, wz_ref[...])).astype(BF16)
    ab_ref[0] = _dot(xn, wab_ref[...])
    outs = (qa_ref, ka_ref, va_ref)
    t_weights = (wfqt_ref, wfkt_ref, wfvt_ref)
    t_proj = []
    for seg in range(3):
        lo = seg * HEADS_W
        u = _dot(xn, wqkv_ref[:, lo:lo + HEADS_W])
        ext_ref[CONV_PAD:CONV_PAD + tm, lo:lo + HEADS_W] = u
        t_proj.append(_dot_nt(t_weights[seg][...], xn))
        y = convw_ref[DN_CONV - 1:DN_CONV, lo:lo + HEADS_W] * u
        for j in range(DN_CONV - 1):
            off = CONV_PAD - (DN_CONV - 1) + j
            y = y + convw_ref[j:j + 1, lo:lo + HEADS_W] * ext_ref[off:off + tm, lo:lo + HEADS_W]
        y = _silu(y)
        for h in range(DN_HEADS):
            blk = y[:, h * DN_HEAD:(h + 1) * DN_HEAD]
            if seg < 2:
                ss = jnp.sum(blk * blk, axis=-1, keepdims=True)
                scale = lax.rsqrt(ss + EPS)
                if seg == 0:
                    scale = scale * (DN_HEAD ** -0.5)
                blk = blk * scale
            outs[seg][0, :, h * DN_HEAD:(h + 1) * DN_HEAD] = blk.astype(BF16)

    n_blk = tm // ATT_BLK
    pos = pos_ref[0].astype(F32)
    qt, kt, vt = t_proj
    for i in range(n_blk):
        sl = slice(i * ATT_BLK, (i + 1) * ATT_BLK)
        ang = pos[:, sl] * invf_ref[...]
        cos = jnp.cos(ang)
        sin = jnp.sin(ang)
        q_i = _qk_norm_rope(qt[:, sl], qnw_ref[...] * (DF_HEAD ** -0.5 * LOG2E), cos, sin)
        k_i = _qk_norm_rope(kt[:, sl], knw_ref[...], cos, sin)
        qt_ref[0, i] = q_i.astype(BF16)
        kb_ref[0, sl, :] = k_i.T.astype(BF16)
        ones_rows = (lax.broadcasted_iota(jnp.int32, (BF16_SUBLANES, ATT_BLK), 0) == 0).astype(BF16)
        for h in range(DF_HEADS):
            vt_ref[0, i, h * DV_AUG:h * DV_AUG + DV, :] = vt[h * DV:(h + 1) * DV, sl].astype(BF16)
            vt_ref[0, i, h * DV_AUG + DV:(h + 1) * DV_AUG, :] = ones_rows


def _in_proj(x, pos3, n1w, wqkv, wz, wab, wfqt, wfkt, wfvt, convw, qnw, knw, invf):
    B, S, D = x.shape
    tm = TOK_TILE
    n_blk = tm // ATT_BLK
    grid = (B, S // tm)
    full = lambda a: pl.BlockSpec(a.shape, lambda b, s: (0,) * a.ndim)
    tok = lambda w: pl.BlockSpec((1, tm, w), lambda b, s: (b, s, 0))
    tpose = pl.BlockSpec((1, n_blk, HEADS_W, ATT_BLK), lambda b, s: (b, s, 0, 0))
    slab = jax.ShapeDtypeStruct((B, S, HEADS_W), BF16)
    slab_t = jax.ShapeDtypeStruct((B, S // ATT_BLK, HEADS_W, ATT_BLK), BF16)
    v_rows = DF_HEADS * DV_AUG
    tpose_v = pl.BlockSpec((1, n_blk, v_rows, ATT_BLK), lambda b, s: (b, s, 0, 0))
    slab_tv = jax.ShapeDtypeStruct((B, S // ATT_BLK, v_rows, ATT_BLK), BF16)
    return pl.pallas_call(
        _in_proj_kernel,
        grid=grid,
        in_specs=[tok(D), pl.BlockSpec((1, 1, tm), lambda b, s: (b, 0, s)), full(n1w), full(wqkv),
                  full(wz), full(wab), full(wfqt), full(wfkt), full(wfvt), full(convw), full(qnw),
                  full(knw), full(invf)],
        out_specs=[tok(HEADS_W), tok(HEADS_W), tok(HEADS_W), tok(HEADS_W), tok(LANES),
                   tpose, tok(HEADS_W), tpose_v],
        out_shape=[slab, slab, slab, slab, jax.ShapeDtypeStruct((B, S, LANES), F32),
                   slab_t, slab, slab_tv],
        scratch_shapes=[pltpu.VMEM((tm + CONV_PAD, 3 * HEADS_W), F32)],
        compiler_params=pltpu.CompilerParams(
            dimension_semantics=("arbitrary", "arbitrary"), vmem_limit_bytes=VMEM_LIMIT_PROJ),
        name="in_proj",
    )(x, pos3, n1w, wqkv, wz, wab, wfqt, wfkt, wfvt, convw, qnw, knw, invf)


def _unit_lower_inverse(mats, row, col):
    c = mats[0].shape[0]
    eye = (row == col).astype(F32)

    def same_block(size):
        return (row // size) == (col // size)

    base = same_block(INV_BASE)
    ps = [jnp.where(base, a, 0.0) for a in mats]
    ns = [eye - d for d in ps]
    width = 2
    while width < INV_BASE:
        pbs = [p.astype(BF16) for p in ps]
        ps = [_dot(pb, pb) for pb in pbs]
        ns = [n + _dot(n.astype(BF16), p.astype(BF16)) for n, p in zip(ns, ps)]
        width *= 2
    size = INV_BASE
    while size < c:
        join = same_block(2 * size) & jnp.logical_not(same_block(size))
        nbs = [n.astype(BF16) for n in ns]
        nes = [_dot(nb, jnp.where(join, a, 0.0).astype(BF16)) for nb, a in zip(nbs, mats)]
        ns = [n - _dot(ne.astype(BF16), nb) for n, ne, nb in zip(ns, nes, nbs)]
        size *= 2
    return [n - eye for n in ns]


def _deltanet_kernel(q_ref, k_ref, v_ref, z_ref, ab_ref, alog_ref, dtb_ref, nw_ref, o_ref, state_ref):
    tc = q_ref.shape[1]
    c = DN_CHUNK

    @pl.when(pl.program_id(1) == 0)
    def _():
        state_ref[...] = jnp.zeros(state_ref.shape, F32)

    n_seq = q_ref.shape[0]
    r_t = lax.broadcasted_iota(jnp.int32, (tc, tc), 0)
    c_t = lax.broadcasted_iota(jnp.int32, (tc, tc), 1)
    tril_bd = ((r_t >= c_t) & ((r_t // c) == (c_t // c))).astype(BF16)
    gcum_b, beta_b, gcum_t = [], [], []
    for bi in range(n_seq):
        ab = ab_ref[bi]
        g = -jnp.exp(alog_ref[...]) * _softplus(ab + dtb_ref[...])
        beta = jax.nn.sigmoid(ab)
        gcum = _split_dot_left(tril_bd, g)
        gcum_b.append(jnp.concatenate(
            [jnp.broadcast_to(gcum[:, h:h + 1], (tc, DN_HEAD)) for h in range(DN_HEADS)], axis=1))
        beta_b.append(jnp.concatenate(
            [jnp.broadcast_to(beta[:, DN_HEADS + h:DN_HEADS + h + 1], (tc, DN_HEAD))
             for h in range(DN_HEADS)], axis=1))
        gcum_t.append(gcum.T)

    row = lax.broadcasted_iota(jnp.int32, (c, c), 0)
    col = lax.broadcasted_iota(jnp.int32, (c, c), 1)
    lower = row >= col
    strict = row > col

    chains = [(bi, h) for bi in range(n_seq) for h in range(DN_HEADS)]
    tiles = [(bi, slice(ci * c, (ci + 1) * c), h, slice(h * DN_HEAD, (h + 1) * DN_HEAD))
             for ci in range(tc // c) for bi, h in chains]
    qs = [q_ref[bi, rs, hs] for bi, rs, _, hs in tiles]
    ks = [k_ref[bi, rs, hs] for bi, rs, _, hs in tiles]
    g_cols = [gcum_b[bi][rs, hs] for bi, rs, _, hs in tiles]
    b_cols = [beta_b[bi][rs, hs] for bi, rs, _, hs in tiles]
    decays = [jnp.where(lower, jnp.exp(jnp.minimum(g_col - gcum_t[bi][h:h + 1, rs], 0.0)), 0.0)
              for g_col, (bi, rs, h, _) in zip(g_cols, tiles)]
    k_betas = [k.astype(F32) * b_col for k, b_col in zip(ks, b_cols)]
    a_mats = [jnp.where(strict, _dot_nt(kb.astype(BF16), k) * dec, 0.0)
              for kb, k, dec in zip(k_betas, ks, decays)]
    intras = [(_dot_nt(q, k) * dec).astype(BF16) for q, k, dec in zip(qs, ks, decays)]
    t_offs = _unit_lower_inverse(a_mats, row, col)
    e_gs = [jnp.exp(g_col) for g_col in g_cols]
    rhss = [jnp.concatenate([v_ref[bi, rs, hs].astype(F32) * b_col, kb * e_g], axis=1)
            for (bi, rs, _, hs), b_col, kb, e_g in zip(tiles, b_cols, k_betas, e_gs)]
    sols = [rhs + _dot(t.astype(BF16), rhs.astype(BF16)) for t, rhs in zip(t_offs, rhss)]
    q_decs = [(q.astype(F32) * e_g).astype(BF16) for q, e_g in zip(qs, e_gs)]
    g_lasts = [g_col[c - 1:c, :] for g_col in g_cols]
    k_decs = [(k.astype(F32) * jnp.exp(g_last - g_col)).astype(BF16)
              for k, g_last, g_col in zip(ks, g_lasts, g_cols)]

    n_ch = len(chains)
    for ci in range(tc // c):
        idx = [ci * n_ch + j for j in range(n_ch)]
        states = [state_ref[bi, h] for bi, h in chains]
        state_bs = [st.astype(BF16) for st in states]
        v_news = [sols[i][:, :DN_HEAD] - _dot(sols[i][:, DN_HEAD:].astype(BF16), sb)
                  for i, sb in zip(idx, state_bs)]
        inter = [_dot(q_decs[i], sb) for i, sb in zip(idx, state_bs)]
        v_new_bs = [vn.astype(BF16) for vn in v_news]
        outs = [o + _dot(intras[i], vb) for i, o, vb in zip(idx, inter, v_new_bs)]
        for j, i in enumerate(idx):
            bi, h = chains[j]
            state_ref[bi, h] = states[j] * jnp.exp(g_lasts[i]) + _dot_tn(k_decs[i], v_new_bs[j])
        for j, i in enumerate(idx):
            bi, rs, _, hs = tiles[i]
            o = outs[j]
            ms = jnp.mean(o * o, axis=-1, keepdims=True)
            o = o * lax.rsqrt(ms + EPS) * nw_ref[...]
            o_ref[bi, rs, hs] = (o * z_ref[bi, rs, hs].astype(F32)).astype(BF16)


def _split_dot_left(m_bf16, a):
    acc = None
    rem = a
    for _ in range(3):
        part = rem.astype(BF16)
        d = _dot(m_bf16, part)
        acc = d if acc is None else acc + d
        rem = rem - part.astype(F32)
    return acc


def _deltanet(qa, ka, va, zs, ab, alog_n, dtb_n, nw):
    B, S, _ = qa.shape
    tc = DN_TILE
    n_seq = DN_SEQS if B % DN_SEQS == 0 else 1
    tok = lambda w: pl.BlockSpec((n_seq, tc, w), lambda b, s: (b, s, 0))
    full = lambda a: pl.BlockSpec(a.shape, lambda b, s: (0,) * a.ndim)
    return pl.pallas_call(
        _deltanet_kernel,
        grid=(B // n_seq, S // tc),
        in_specs=[tok(HEADS_W), tok(HEADS_W), tok(HEADS_W), tok(HEADS_W), tok(LANES),
                  full(alog_n), full(dtb_n), full(nw)],
        out_specs=tok(HEADS_W),
        out_shape=jax.ShapeDtypeStruct((B, S, HEADS_W), BF16),
        scratch_shapes=[pltpu.VMEM((n_seq, DN_HEADS, DN_HEAD, DN_HEAD), F32)],
        compiler_params=pltpu.CompilerParams(
            dimension_semantics=("arbitrary", "arbitrary"), vmem_limit_bytes=VMEM_LIMIT_DN),
        name="deltanet",
    )(qa, ka, va, zs, ab, alog_n, dtb_n, nw)


def _diff_attn_kernel(qt_ref, k_ref, vt_ref, lq1_ref, lk1_ref, lq2_ref, lk2_ref, sw_ref, o_ref,
                      pa_ref, pb_ref, ala_ref, alb_ref, m_ref, acc_ref, *, lam_init):
    blk = ATT_BLK
    n_sub = ATT_QBLK // ATT_BLK
    iq = pl.program_id(2)
    hd = 2 * DF_HEAD

    q_maps = []
    for hh in range(ATT_HEADS_PER_STEP):
        qt = jnp.concatenate([qt_ref[0, i, hh * hd:(hh + 1) * hd, :] for i in range(n_sub)], axis=1)
        d_idx = lax.broadcasted_iota(jnp.int32, qt.shape, 0)
        zero = jnp.zeros_like(qt)
        q_maps.append(jnp.where(d_idx < DF_HEAD, qt, zero))
        q_maps.append(jnp.where(d_idx >= DF_HEAD, qt, zero))

    def probs(j, p_ref, al_ref, mask_lane0=None):
        lane0 = mask_lane0 or 0
        rows = pl.ds(pl.multiple_of(j * blk, blk), blk)
        for hh in range(ATT_HEADS_PER_STEP):
            k_blk = k_ref[0, rows, hh * hd:(hh + 1) * hd]
            for m_i in range(2):
                c = 2 * hh + m_i
                s = _dot(k_blk, q_maps[c][:, lane0:])
                if mask_lane0 is not None:
                    kpos = lax.broadcasted_iota(jnp.int32, s.shape, 0)
                    qpos = lax.broadcasted_iota(jnp.int32, s.shape, 1)
                    s = jnp.where(kpos <= qpos, s, -jnp.inf)
                s = s.astype(BF16)
                m_old = m_ref[c, :, lane0:]
                m_new = jnp.maximum(m_old, jnp.max(s, axis=0, keepdims=True).astype(F32))
                m_ref[c, :, lane0:] = m_new
                al_ref[c, :, lane0:] = jnp.exp2(m_old - m_new)
                p_ref[c, :, lane0:] = jnp.exp2(s - m_new.astype(BF16))

    def values(j, p_ref, al_ref, lane0=0):
        for hh in range(ATT_HEADS_PER_STEP):
            v_blk = vt_ref[0, j, hh * DV_AUG:(hh + 1) * DV_AUG, :]
            for m_i in range(2):
                c = 2 * hh + m_i
                acc_ref[c, :, lane0:] = (al_ref[c, :, lane0:] * acc_ref[c, :, lane0:]
                                         + _dot(v_blk, p_ref[c, :, lane0:]))

    m_ref[...] = jnp.full(m_ref.shape, -jnp.inf, F32)
    acc_ref[...] = jnp.zeros(acc_ref.shape, F32)
    diag0 = n_sub * iq
    bufs = ((pa_ref, ala_ref), (pb_ref, alb_ref))

    def drain(first_pending, blocks):
        pending = first_pending
        for j, par, lane0 in blocks:
            probs(j, *bufs[par], mask_lane0=lane0)
            if pending is not None:
                values(pending[0], *bufs[pending[1]], lane0=pending[2] or 0)
            pending = (j, par, lane0)
        values(pending[0], *bufs[pending[1]], lane0=pending[2] or 0)

    diag_blocks = [(diag0 + d, d % 2, d * blk) for d in range(n_sub)]

    @pl.when(iq == 0)
    def _():
        drain(None, diag_blocks)

    @pl.when(iq > 0)
    def _():
        probs(0, pa_ref, ala_ref)

        def pair(t, _):
            a = 2 * t
            probs(a + 1, pb_ref, alb_ref)
            values(a, pa_ref, ala_ref)
            probs(a + 2, pa_ref, ala_ref)
            values(a + 1, pb_ref, alb_ref)
            return 0

        lax.fori_loop(0, (n_sub // 2) * iq - 1, pair, 0)

        drain((diag0 - 2, 0, None), [(diag0 - 1, 1, None)] + diag_blocks)

    lam = (jnp.exp(jnp.sum(lq1_ref[...] * lk1_ref[...], axis=-1, keepdims=True))
           - jnp.exp(jnp.sum(lq2_ref[...] * lk2_ref[...], axis=-1, keepdims=True)) + lam_init)
    for hh in range(ATT_HEADS_PER_STEP):
        a1 = acc_ref[2 * hh]
        a2 = acc_ref[2 * hh + 1]
        inv1 = 1.0 / a1[DV:DV + 1]
        inv2 = lam / a2[DV:DV + 1]
        o = a1[:DV] * inv1 - a2[:DV] * inv2
        ms = jnp.mean(o * o, axis=0, keepdims=True)
        o = o * lax.rsqrt(ms + EPS) * sw_ref[...] * (1.0 - lam_init)
        o_ref[0, :, hh * DV:(hh + 1) * DV] = o.T.astype(BF16)


def _diff_attn(qt, kb, vt, lq1, lk1, lq2, lk2, sw, lam_init):
    B, S, _ = kb.shape
    nb = S // ATT_BLK
    hps = ATT_HEADS_PER_STEP
    chains = 2 * hps
    full = lambda a: pl.BlockSpec(a.shape, lambda b, g, i: (0,) * a.ndim)
    return pl.pallas_call(
        functools.partial(_diff_attn_kernel, lam_init=lam_init),
        grid=(B, DF_HEADS // hps, S // ATT_QBLK),
        in_specs=[pl.BlockSpec((1, ATT_QBLK // ATT_BLK, hps * DV, ATT_BLK),
                               lambda b, g, i: (b, i, g, 0)),
                  pl.BlockSpec((1, S, hps * DV), lambda b, g, i: (b, 0, g)),
                  pl.BlockSpec((1, nb, hps * DV_AUG, ATT_BLK), lambda b, g, i: (b, 0, g, 0)),
                  full(lq1), full(lk1), full(lq2), full(lk2), full(sw)],
        out_specs=pl.BlockSpec((1, ATT_QBLK, hps * DV), lambda b, g, i: (b, i, g)),
        out_shape=jax.ShapeDtypeStruct((B, S, HEADS_W), BF16),
        scratch_shapes=[pltpu.VMEM((chains, ATT_BLK, ATT_QBLK), BF16),
                        pltpu.VMEM((chains, ATT_BLK, ATT_QBLK), BF16),
                        pltpu.VMEM((chains, 1, ATT_QBLK), F32),
                        pltpu.VMEM((chains, 1, ATT_QBLK), F32),
                        pltpu.VMEM((chains, 1, ATT_QBLK), F32),
                        pltpu.VMEM((chains, DV_AUG, ATT_QBLK), F32)],
        compiler_params=pltpu.CompilerParams(
            dimension_semantics=("arbitrary", "arbitrary", "arbitrary"),
            vmem_limit_bytes=VMEM_LIMIT_ATT),
        name="diff_attn",
    )(qt, kb, vt, lq1, lk1, lq2, lk2, sw)


def _out_ffn_kernel(x_ref, oa_ref, ob_ref, wout_ref, n2w_ref, wup_ref, cw_ref, cb_ref, wdown_ref,
                    out_ref, ext_ref, act_ref):
    tm = x_ref.shape[1]
    ffn = wdown_ref.shape[0]
    fb = FFN_BLK
    s_idx = pl.program_id(1)

    @pl.when(s_idx == 0)
    def _():
        ext_ref[0:CONV_PAD, :] = jnp.zeros((CONV_PAD, ext_ref.shape[1]), F32)

    @pl.when(s_idx > 0)
    def _():
        ext_ref[0:CONV_PAD, :] = ext_ref[tm:tm + CONV_PAD, :]

    mix = jnp.concatenate([oa_ref[0], ob_ref[0]], axis=-1)
    h1 = x_ref[0] + _dot(mix, wout_ref[...])
    ms = jnp.mean(h1 * h1, axis=-1, keepdims=True)
    hn = (h1 * lax.rsqrt(ms + EPS) * n2w_ref[...]).astype(BF16)

    ext_ref[CONV_PAD:CONV_PAD + tm, :] = _dot(hn, wup_ref[...])

    def conv_cols(lo):
        cs = slice(lo, lo + fb)
        y = cb_ref[:, cs]
        for j in range(FFN_CONV):
            off = CONV_PAD - (FFN_CONV - 1) + j
            y = y + cw_ref[j:j + 1, cs] * ext_ref[off:off + tm, cs]
        return y

    out_ref[0] = h1
    for ci in range(ffn // fb):
        gate = conv_cols(ci * fb)
        up = conv_cols(ffn + ci * fb)
        act = (_silu(gate) * up).astype(BF16)
        out_ref[0] += _dot(act, wdown_ref[ci * fb:(ci + 1) * fb, :])


def _out_ffn(x, oa, ob, wout, n2w, wup, cw, cb, wdown):
    B, S, D = x.shape
    tm = FFN_TOK_TILE
    ffn = wdown.shape[0]
    tok = lambda w: pl.BlockSpec((1, tm, w), lambda b, s: (b, s, 0))
    full = lambda a: pl.BlockSpec(a.shape, lambda b, s: (0,) * a.ndim,
                                  pipeline_mode=pl.Buffered(1))
    return pl.pallas_call(
        _out_ffn_kernel,
        grid=(B, S // tm),
        in_specs=[tok(D), tok(HEADS_W), tok(HEADS_W), full(wout), full(n2w), full(wup), full(cw),
                  full(cb), full(wdown)],
        out_specs=tok(D),
        out_shape=jax.ShapeDtypeStruct((B, S, D), F32),
        scratch_shapes=[pltpu.VMEM((tm + CONV_PAD, 2 * ffn), F32),
                        pltpu.VMEM((tm, ffn), BF16)],
        compiler_params=pltpu.CompilerParams(
            dimension_semantics=("arbitrary", "arbitrary"), vmem_limit_bytes=VMEM_LIMIT_FFN),
        name="out_ffn",
    )(x, oa, ob, wout, n2w, wup, cw, cb, wdown)


def _lane_place(v, offset):
    return jnp.zeros((1, LANES), F32).at[0, offset:offset + v.shape[0]].set(v.astype(F32))


def kernel(x, positions, norm1_w, w_in, dn_conv_w, dn_a_log, dn_dt_bias, dn_norm_w, df_q_norm_w,
           df_k_norm_w, df_lambda_q1, df_lambda_k1, df_lambda_q2, df_lambda_k2, df_subln_w, w_out,
           norm2_w, w_up, ffn_conv_w, ffn_conv_b, w_down):
    B, S, D = x.shape
    depth = w_in.shape[0]
    w = HEADS_W
    h = x
    pos3 = positions.reshape(B, 1, S)
    inv_freq = ROPE_THETA ** (-jnp.arange(0, ROPE_DIM, 2, dtype=F32) / ROPE_DIM)
    invf = jnp.broadcast_to(inv_freq[:, None], (ROPE_HALF, ATT_BLK))
    for l in range(depth):
        lam_init = 0.8 - 0.6 * math.exp(-0.3 * l)
        wi = w_in[l].astype(BF16)
        wqkv = wi[:, 0:3 * w]
        wz = wi[:, 3 * w:4 * w]
        n_gate = 2 * DN_HEADS
        wab = jnp.pad(wi[:, 4 * w:4 * w + n_gate], ((0, 0), (0, LANES - n_gate)))
        base = 4 * w + n_gate
        wfqt = wi[:, base:base + w].T
        wfkt = wi[:, base + w:base + 2 * w].T
        wfvt = wi[:, base + 2 * w:base + 3 * w].T
        qnw = jnp.broadcast_to(df_q_norm_w[l].astype(F32)[:, None], (DF_HEAD, ATT_BLK))
        knw = jnp.broadcast_to(df_k_norm_w[l].astype(F32)[:, None], (DF_HEAD, ATT_BLK))

        qa, ka, va, zs, ab, qt, kb, vt = _in_proj(
            h, pos3, norm1_w[l][None, :], wqkv, wz, wab, wfqt, wfkt, wfvt, dn_conv_w[l], qnw, knw,
            invf)

        o_a = _deltanet(qa, ka, va, zs, ab, _lane_place(dn_a_log[l], 0),
                        _lane_place(dn_dt_bias[l], 0), dn_norm_w[l][None, :].astype(F32))

        sw = jnp.broadcast_to(df_subln_w[l].astype(F32)[:, None], (2 * DF_HEAD, ATT_QBLK))
        o_b = _diff_attn(qt, kb, vt, df_lambda_q1[l][None, :], df_lambda_k1[l][None, :],
                         df_lambda_q2[l][None, :], df_lambda_k2[l][None, :], sw, lam_init)

        h = _out_ffn(h, o_a, o_b, w_out[l].astype(BF16), norm2_w[l][None, :],
                     w_up[l].astype(BF16), ffn_conv_w[l], ffn_conv_b[l][None, :],
                     w_down[l].astype(BF16))
    return h
```
